```python
import jax
import jax.numpy as jnp
from jax import lax
import numpy as np

D_MODEL = 4096
BATCH = 1
SEQ = 8192
DEPTH = 4

HEAD_DIM = 128
N_META = 16
GRID_W = 64
BLOCK = 128
WINDOW = 128
RMS_EPS = 1e-6
NEG_INF = -1e30

D_FF = (3 * D_MODEL) // 2

LRU_WIDTH = D_MODEL // 2
LRU_BLOCKS = LRU_WIDTH // HEAD_DIM
LRU_BLOCK_DIM = LRU_WIDTH // LRU_BLOCKS
CONV_WIDTH = 4
CONV_PAD_LEFT = 2
LRU_C = 8.0

SWA_Q_HEADS = (D_MODEL // 2) // HEAD_DIM
SWA_KV_HEADS = SWA_Q_HEADS // 4
SWA_Q_DIM = SWA_Q_HEADS * HEAD_DIM
SWA_KV_DIM = SWA_KV_HEADS * HEAD_DIM

GA_Q_HEADS = D_MODEL // HEAD_DIM
GA_KV_HEADS = GA_Q_HEADS // 4
GA_Q_DIM = GA_Q_HEADS * HEAD_DIM
GA_KV_DIM = GA_KV_HEADS * HEAD_DIM
ROPE_BASE = 10000.0

AB_IN = 2 * LRU_WIDTH + SWA_Q_DIM + 2 * SWA_KV_DIM
AB_OUT = LRU_WIDTH + SWA_Q_DIM
C_IN = GA_Q_DIM + 2 * GA_KV_DIM
C_OUT = GA_Q_DIM
N_AB = (DEPTH + 1) // 2
N_C = DEPTH // 2
ATTN_SCALE = HEAD_DIM ** -0.5

kernel_name = "hybrid_rglru_swa_axial_macaron_encoder"


def rms_norm(x, g):
    xf = x.astype(jnp.float32)
    y = xf * lax.rsqrt(jnp.mean(xf * xf, axis=-1, keepdims=True) + RMS_EPS)
    return (y * g.astype(jnp.float32)).astype(x.dtype)


def swiglu(x, w_gu, w_down):
    gate, up = jnp.split(x @ w_gu, 2, axis=-1)
    return (jax.nn.silu(gate) * up) @ w_down


def depthwise_conv(u, w, b):
    t = u.shape[1]
    up = jnp.pad(u, ((0, 0), (CONV_PAD_LEFT, CONV_WIDTH - 1 - CONV_PAD_LEFT), (0, 0)))
    out = up[:, 0:t] * w[0]
    for k in range(1, CONV_WIDTH):
        out = out + up[:, k:k + t] * w[k]
    return out + b


def _linear_recurrence_combine(left, right):
    a_l, b_l = left
    a_r, b_r = right
    return a_l * a_r, a_r * b_l + b_r


def rglru_direction(u, w_a, b_a, w_x, b_x, lam, reverse):
    bsz, t, _ = u.shape
    ub = u.reshape(bsz, t, LRU_BLOCKS, LRU_BLOCK_DIM)
    gate_r = jnp.einsum('btnc,ncd->btnd', ub, w_a).reshape(bsz, t, LRU_WIDTH) + b_a
    gate_i = jnp.einsum('btnc,ncd->btnd', ub, w_x).reshape(bsz, t, LRU_WIDTH) + b_x
    r = jax.nn.sigmoid(gate_r.astype(jnp.float32))
    i = jax.nn.sigmoid(gate_i.astype(jnp.float32))
    log_a = -LRU_C * r * jax.nn.softplus(-lam.astype(jnp.float32))
    a = jnp.exp(log_a)
    b = jnp.sqrt(-jnp.expm1(2.0 * log_a)) * (i * u.astype(jnp.float32))
    _, h = lax.associative_scan(_linear_recurrence_combine, (a, b), reverse=reverse, axis=1)
    return h


def sink_attend(q, k, v, bias, sink):
    s = jnp.einsum('bjqhgd,bjkhd->bjhgqk', q, k).astype(jnp.float32) * ATTN_SCALE + bias
    sink_col = jnp.broadcast_to(sink.astype(jnp.float32)[:, :, None, None], s.shape[:-1] + (1,))
    p = jax.nn.softmax(jnp.concatenate([sink_col, s], axis=-1), axis=-1)[..., 1:]
    return jnp.einsum('bjhgqk,bjkhd->bjqhgd', p.astype(v.dtype), v)


def windowed_attention(q, k, v, sink):
    bsz, t, hq, d = q.shape
    hkv = k.shape[2]
    g = hq // hkv
    n = t - N_META
    nb = n // BLOCK
    slopes = jnp.asarray(2.0 ** (-8.0 * np.arange(1, hq + 1) / hq), jnp.float32).reshape(hkv, g)
    sink_g = sink.reshape(hkv, g)
    qm, qr = q[:, :N_META], q[:, N_META:]
    km, kr = k[:, :N_META], k[:, N_META:]
    vm, vr = v[:, :N_META], v[:, N_META:]

    def band(z):
        zp = jnp.pad(z, ((0, 0), (BLOCK, BLOCK), (0, 0), (0, 0))).reshape(bsz, nb + 2, BLOCK, hkv, d)
        return jnp.concatenate([zp[:, :-2], zp[:, 1:-1], zp[:, 2:]], axis=2)

    k_blk = jnp.concatenate([jnp.broadcast_to(km[:, None], (bsz, nb, N_META, hkv, d)), band(kr)], axis=2)
    v_blk = jnp.concatenate([jnp.broadcast_to(vm[:, None], (bsz, nb, N_META, hkv, d)), band(vr)], axis=2)
    qpos = jnp.arange(nb)[:, None] * BLOCK + jnp.arange(BLOCK)[None, :]
    kpos = jnp.arange(nb)[:, None] * BLOCK - BLOCK + jnp.arange(3 * BLOCK)[None, :]
    rel = jnp.abs(qpos[:, :, None] - kpos[:, None, :])
    ok = (rel <= WINDOW) & (kpos[:, None, :] >= 0) & (kpos[:, None, :] < n)
    local_bias = jnp.where(ok[:, None, None],
                           -slopes[None, :, :, None, None] * rel[:, None, None].astype(jnp.float32),
                           NEG_INF)
    meta_bias = jnp.zeros((nb, hkv, g, BLOCK, N_META), jnp.float32)
    bias_r = jnp.concatenate([meta_bias, local_bias], axis=-1)[None]
    o_real = sink_attend(qr.reshape(bsz, nb, BLOCK, hkv, g, d), k_blk, v_blk, bias_r, sink_g)

    k_m = jnp.concatenate([km, kr[:, :BLOCK]], axis=1)[:, None]
    v_m = jnp.concatenate([vm, vr[:, :BLOCK]], axis=1)[:, None]
    key_comb = jnp.arange(N_META + BLOCK)
    ok_m = jnp.abs(key_comb[None, :] - jnp.arange(N_META)[:, None]) <= WINDOW
    bias_m = jnp.where(ok_m, 0.0, NEG_INF).astype(jnp.float32)[None, None, None, None]
    o_meta = sink_attend(qm.reshape(bsz, 1, N_META, hkv, g, d), k_m, v_m, bias_m, sink_g)

    return jnp.concatenate([o_meta.reshape(bsz, N_META, hq * d), o_real.reshape(bsz, n, hq * d)], axis=1)


def dense_attend(qb, k, v):
    s = jnp.einsum('bqhgd,bkhd->bhgqk', qb, k).astype(jnp.float32) * ATTN_SCALE
    p = jax.nn.softmax(s, axis=-1)
    return jnp.einsum('bhgqk,bkhd->bqhgd', p.astype(v.dtype), v)


def rope_rotate(x, ang):
    x1, x2 = jnp.split(x, 2, axis=-1)
    c, s = jnp.cos(ang), jnp.sin(ang)
    return jnp.concatenate([x1 * c - x2 * s, x2 * c + x1 * s], axis=-1)


def apply_axial_rope(x, ang_row, ang_col):
    xr, xc = jnp.split(x.astype(jnp.float32), 2, axis=-1)
    out = jnp.concatenate([rope_rotate(xr, ang_row[None, :, None]),
                           rope_rotate(xc, ang_col[None, :, None])], axis=-1)
    return out.astype(x.dtype)


def axial_angles(rows):
    half = HEAD_DIM // 2
    row = jnp.concatenate([jnp.zeros((N_META,), jnp.int32), jnp.repeat(jnp.arange(rows, dtype=jnp.int32), GRID_W)])
    col = jnp.concatenate([jnp.zeros((N_META,), jnp.int32), jnp.tile(jnp.arange(GRID_W, dtype=jnp.int32), rows)])
    inv_freq = ROPE_BASE ** (-(jnp.arange(half // 2, dtype=jnp.float32) * 2.0 / half))
    return row.astype(jnp.float32)[:, None] * inv_freq, col.astype(jnp.float32)[:, None] * inv_freq


def mixer_ab(h, w_in, w_out, conv_w, conv_b, w_a, b_a, w_x, b_x, lam, sink):
    bsz, t, _ = h.shape
    proj = h @ w_in
    u, gate, q, k, v = jnp.split(proj, [LRU_WIDTH, 2 * LRU_WIDTH, 2 * LRU_WIDTH + SWA_Q_DIM,
                                        2 * LRU_WIDTH + SWA_Q_DIM + SWA_KV_DIM], axis=-1)
    u = depthwise_conv(u, conv_w, conv_b)
    y_lru = (rglru_direction(u, w_a[0], b_a[0], w_x[0], b_x[0], lam[0], False)
             + rglru_direction(u, w_a[1], b_a[1], w_x[1], b_x[1], lam[1], True))
    y_a = (y_lru * jax.nn.gelu(gate.astype(jnp.float32))).astype(h.dtype)
    y_b = windowed_attention(q.reshape(bsz, t, SWA_Q_HEADS, HEAD_DIM),
                             k.reshape(bsz, t, SWA_KV_HEADS, HEAD_DIM),
                             v.reshape(bsz, t, SWA_KV_HEADS, HEAD_DIM), sink)
    return jnp.concatenate([y_a, y_b.astype(h.dtype)], axis=-1) @ w_out


def mixer_c(h, w_in, w_out, q_norm, k_norm, ang_row, ang_col):
    bsz, t, _ = h.shape
    g = GA_Q_HEADS // GA_KV_HEADS
    n = t - N_META
    nb = n // BLOCK
    q, k, v = jnp.split(h @ w_in, [GA_Q_DIM, GA_Q_DIM + GA_KV_DIM], axis=-1)
    q = apply_axial_rope(rms_norm(q.reshape(bsz, t, GA_Q_HEADS, HEAD_DIM), q_norm), ang_row, ang_col)
    k = apply_axial_rope(rms_norm(k.reshape(bsz, t, GA_KV_HEADS, HEAD_DIM), k_norm), ang_row, ang_col)
    v = v.reshape(bsz, t, GA_KV_HEADS, HEAD_DIM)
    q = q.reshape(bsz, t, GA_KV_HEADS, g, HEAD_DIM)
    o_meta = dense_attend(q[:, :N_META], k, v)
    q_blocks = jnp.moveaxis(q[:, N_META:].reshape(bsz, nb, BLOCK, GA_KV_HEADS, g, HEAD_DIM), 1, 0)
    o_real = lax.map(lambda qb: dense_attend(qb, k, v), q_blocks)
    o_real = jnp.moveaxis(o_real, 0, 1).reshape(bsz, n, GA_Q_DIM)
    o = jnp.concatenate([o_meta.reshape(bsz, N_META, GA_Q_DIM), o_real], axis=1)
    return o @ w_out


def setup_inputs(seed: int = 0) -> dict:
    key = jax.random.key(seed)
    ks = jax.random.split(key, 32)
    f32 = jnp.float32

    def nrm(i, shape, scale):
        return jax.random.normal(ks[i], shape, f32) * scale

    def gain(i, shape):
        return 1.0 + 0.02 * jax.random.normal(ks[i], shape, f32)

    a0 = jax.random.uniform(ks[20], (N_AB, 2, LRU_WIDTH), f32, 0.9, 0.999)
    s0 = a0 ** (1.0 / LRU_C)
    lru_lambda = jnp.log(s0) - jnp.log1p(-s0)
    return {
        "x": nrm(0, (BATCH, SEQ, D_MODEL), 1.0),
        "meta_tokens": nrm(1, (N_META, D_MODEL), 1.0),
        "ffn1_norm": gain(2, (DEPTH, D_MODEL)),
        "ffn1_w_gu": nrm(3, (DEPTH, D_MODEL, 2 * D_FF), D_MODEL ** -0.5),
        "ffn1_w_down": nrm(4, (DEPTH, D_FF, D_MODEL), D_FF ** -0.5),
        "mix_norm": gain(5, (DEPTH, D_MODEL)),
        "ab_w_in": nrm(6, (N_AB, D_MODEL, AB_IN), D_MODEL ** -0.5),
        "ab_w_out": nrm(7, (N_AB, AB_OUT, D_MODEL), AB_OUT ** -0.5),
        "lru_conv_w": nrm(8, (N_AB, CONV_WIDTH, LRU_WIDTH), CONV_WIDTH ** -0.5),
        "lru_conv_b": nrm(9, (N_AB, LRU_WIDTH), 0.02),
        "lru_w_a": nrm(10, (N_AB, 2, LRU_BLOCKS, LRU_BLOCK_DIM, LRU_BLOCK_DIM), LRU_BLOCK_DIM ** -0.5),
        "lru_b_a": nrm(11, (N_AB, 2, LRU_WIDTH), 0.02),
        "lru_w_x": nrm(12, (N_AB, 2, LRU_BLOCKS, LRU_BLOCK_DIM, LRU_BLOCK_DIM), LRU_BLOCK_DIM ** -0.5),
        "lru_b_x": nrm(13, (N_AB, 2, LRU_WIDTH), 0.02),
        "lru_lambda": lru_lambda,
        "swa_sink": nrm(14, (N_AB, SWA_Q_HEADS), 0.5),
        "c_w_in": nrm(15, (N_C, D_MODEL, C_IN), D_MODEL ** -0.5),
        "c_w_out": nrm(16, (N_C, C_OUT, D_MODEL), C_OUT ** -0.5),
        "c_q_norm": gain(17, (N_C, HEAD_DIM)),
        "c_k_norm": gain(18, (N_C, HEAD_DIM)),
        "ffn2_norm": gain(19, (DEPTH, D_MODEL)),
        "ffn2_w_gu": nrm(21, (DEPTH, D_MODEL, 2 * D_FF), D_MODEL ** -0.5),
        "ffn2_w_down": nrm(22, (DEPTH, D_FF, D_MODEL), D_FF ** -0.5),
        "final_norm": gain(23, (D_MODEL,)),
    }


def reference(x, meta_tokens, ffn1_norm, ffn1_w_gu, ffn1_w_down, mix_norm, ab_w_in, ab_w_out,
              lru_conv_w, lru_conv_b, lru_w_a, lru_b_a, lru_w_x, lru_b_x, lru_lambda, swa_sink,
              c_w_in, c_w_out, c_q_norm, c_k_norm, ffn2_norm, ffn2_w_gu, ffn2_w_down, final_norm):
    bsz, n, d = x.shape
    rows = n // GRID_W
    ang_row, ang_col = axial_angles(rows)
    h = jnp.concatenate([jnp.broadcast_to(meta_tokens.astype(x.dtype)[None], (bsz, N_META, d)), x], axis=1)
    for layer in range(DEPTH):
        h = h + 0.5 * swiglu(rms_norm(h, ffn1_norm[layer]), ffn1_w_gu[layer], ffn1_w_down[layer])
        hn = rms_norm(h, mix_norm[layer])
        i = layer // 2
        if layer % 2 == 0:
            h = h + mixer_ab(hn, ab_w_in[i], ab_w_out[i], lru_conv_w[i], lru_conv_b[i], lru_w_a[i],
                             lru_b_a[i], lru_w_x[i], lru_b_x[i], lru_lambda[i], swa_sink[i])
        else:
            h = h + mixer_c(hn, c_w_in[i], c_w_out[i], c_q_norm[i], c_k_norm[i], ang_row, ang_col)
        h = h + 0.5 * swiglu(rms_norm(h, ffn2_norm[layer]), ffn2_w_gu[layer], ffn2_w_down[layer])
    return rms_norm(h[:, N_META:], final_norm)
```

```python
import functools

import numpy as np
import jax
import jax.numpy as jnp
from jax import lax
from jax.experimental import pallas as pl
from jax.experimental.pallas import tpu as pltpu

F32 = jnp.float32
BF16 = jnp.bfloat16

D_MODEL = 4096
SEQ = 8192
DEPTH = 4
HEAD_DIM = 128
N_META = 16
T_TOK = SEQ + N_META
GRID_W = 64
BLOCK = 128
WINDOW = 128
RMS_EPS = 1e-6
NEG_INF = -1e30
D_FF = (3 * D_MODEL) // 2
LRU_WIDTH = D_MODEL // 2
LRU_C = 8.0
SWA_Q_HEADS = 16
SWA_KV_HEADS = 4
SWA_Q_DIM = SWA_Q_HEADS * HEAD_DIM
SWA_KV_DIM = SWA_KV_HEADS * HEAD_DIM
GA_Q_HEADS = 32
GA_KV_HEADS = 8
GA_Q_DIM = GA_Q_HEADS * HEAD_DIM
GA_KV_DIM = GA_KV_HEADS * HEAD_DIM
GQA_GROUP = 4
ROPE_BASE = 10000.0
ATTN_SCALE = HEAD_DIM ** -0.5

V7X_VMEM_BYTES = 64 * 2**20
VMEM_CAP_BYTES = V7X_VMEM_BYTES - 6 * 2**20

TM = 912
KEYS_PAD = 8320
DENSE_TK = 640
DENSE_TQ = 304
PREP_TM = 320
NORM_TM = 432
LRU_L = 432


def _nbytes(shape, dtype):
    return int(np.prod(shape)) * jnp.dtype(dtype).itemsize


def _params(block_bytes, extra_bytes=0):
    need = 2 * block_bytes + extra_bytes + 4 * 2**20
    return pltpu.CompilerParams(vmem_limit_bytes=int(min(max(need, 16 * 2**20), VMEM_CAP_BYTES)))


def _rmsnorm_kernel(x_ref, g_ref, o_ref):
    x = x_ref[...]
    y = x * lax.rsqrt(jnp.mean(x * x, axis=-1, keepdims=True) + RMS_EPS)
    o_ref[...] = (y * g_ref[...]).astype(o_ref.dtype)


def rmsnorm(x, g, tm=NORM_TM):
    m, d = x.shape
    blk = _nbytes((tm, d), F32) + _nbytes((tm, d), BF16)
    return pl.pallas_call(
        _rmsnorm_kernel,
        grid=(m // tm,),
        in_specs=[pl.BlockSpec((tm, d), lambda i: (i, 0)),
                  pl.BlockSpec((1, d), lambda i: (0, 0))],
        out_specs=pl.BlockSpec((tm, d), lambda i: (i, 0)),
        out_shape=jax.ShapeDtypeStruct((m, d), BF16),
        compiler_params=_params(blk, _nbytes((tm, d), F32)),
        name="rmsnorm",
    )(x, g.reshape(1, d))


def _final_norm_kernel(a_ref, b_ref, g_ref, o_ref):
    x = jnp.concatenate([a_ref[N_META:, :], b_ref[:N_META, :]], axis=0)
    y = x * lax.rsqrt(jnp.mean(x * x, axis=-1, keepdims=True) + RMS_EPS)
    o_ref[...] = y * g_ref[...]


def final_rmsnorm(h, g, tm=256):
    m, d = h.shape
    n = m - N_META
    blk = 3 * _nbytes((tm, d), F32)
    return pl.pallas_call(
        _final_norm_kernel,
        grid=(n // tm,),
        in_specs=[pl.BlockSpec((tm, d), lambda i: (i, 0)),
                  pl.BlockSpec((tm, d), lambda i: (i + 1, 0)),
                  pl.BlockSpec((1, d), lambda i: (0, 0))],
        out_specs=pl.BlockSpec((tm, d), lambda i: (i, 0)),
        out_shape=jax.ShapeDtypeStruct((n, d), F32),
        compiler_params=_params(blk, 2 * _nbytes((tm, d), F32)),
        name="final_norm",
    )(h, h, g.reshape(1, d))


def _dot(a, b):
    return jnp.dot(a, b, preferred_element_type=F32)


def _mm_kernel(x_ref, w_ref, o_ref):
    o_ref[...] = _dot(x_ref[...], w_ref[...]).astype(o_ref.dtype)


def matmul(x, w, layer, n_off, n, out_dtype, tn, tm=TM):
    m, k = x.shape
    joff = n_off // tn
    blk = _nbytes((tm, k), BF16) + _nbytes((k, tn), BF16) + _nbytes((tm, tn), out_dtype)
    return pl.pallas_call(
        _mm_kernel,
        grid=(m // tm, n // tn),
        in_specs=[pl.BlockSpec((tm, k), lambda i, j: (i, 0)),
                  pl.BlockSpec((None, k, tn), lambda i, j: (layer, 0, j + joff))],
        out_specs=pl.BlockSpec((tm, tn), lambda i, j: (i, j)),
        out_shape=jax.ShapeDtypeStruct((m, n), out_dtype),
        compiler_params=_params(blk, _nbytes((tm, tn), F32)),
        name="matmul",
    )(x, w)


def _mm_swiglu_kernel(x_ref, wg_ref, wu_ref, o_ref):
    x = x_ref[...]
    gate = _dot(x, wg_ref[...])
    up = _dot(x, wu_ref[...])
    o_ref[...] = (jax.nn.silu(gate) * up).astype(o_ref.dtype)


def matmul_swiglu(x, w_gu, layer, tn=512, tm=TM):
    m, k = x.shape
    f = w_gu.shape[2] // 2
    uoff = f // tn
    blk = _nbytes((tm, k), BF16) + 2 * _nbytes((k, tn), BF16) + _nbytes((tm, tn), BF16)
    return pl.pallas_call(
        _mm_swiglu_kernel,
        grid=(m // tm, f // tn),
        in_specs=[pl.BlockSpec((tm, k), lambda i, j: (i, 0)),
                  pl.BlockSpec((None, k, tn), lambda i, j: (layer, 0, j)),
                  pl.BlockSpec((None, k, tn), lambda i, j: (layer, 0, j + uoff))],
        out_specs=pl.BlockSpec((tm, tn), lambda i, j: (i, j)),
        out_shape=jax.ShapeDtypeStruct((m, f), BF16),
        compiler_params=_params(blk, 3 * _nbytes((tm, tn), F32)),
        name="matmul_swiglu",
    )(x, w_gu, w_gu)


def _mm_res_kernel(x_ref, w_ref, r_ref, o_ref, *, scale):
    o_ref[...] = r_ref[...] + scale * _dot(x_ref[...], w_ref[...])


def matmul_residual(x, w, layer, res, scale, tn=512, tm=TM):
    m, k = x.shape
    n = w.shape[2]
    blk = _nbytes((tm, k), BF16) + _nbytes((k, tn), BF16) + 2 * _nbytes((tm, tn), F32)
    return pl.pallas_call(
        functools.partial(_mm_res_kernel, scale=scale),
        grid=(m // tm, n // tn),
        in_specs=[pl.BlockSpec((tm, k), lambda i, j: (i, 0)),
                  pl.BlockSpec((None, k, tn), lambda i, j: (layer, 0, j)),
                  pl.BlockSpec((tm, tn), lambda i, j: (i, j))],
        out_specs=pl.BlockSpec((tm, tn), lambda i, j: (i, j)),
        out_shape=jax.ShapeDtypeStruct((m, n), F32),
        compiler_params=_params(blk, _nbytes((tm, tn), F32)),
        name="matmul_residual",
    )(x, w, res)


def _mm2_res_kernel(xa_ref, xb_ref, wa_ref, wb_ref, r_ref, o_ref):
    acc = _dot(xa_ref[...], wa_ref[...]) + _dot(xb_ref[...], wb_ref[...])
    o_ref[...] = r_ref[...] + acc


def matmul2_residual(xa, xb, w, layer, res, tn=512, tm=TM):
    m, ka = xa.shape
    kb = xb.shape[1]
    assert ka == kb
    n = w.shape[2]
    blk = 2 * _nbytes((tm, ka), BF16) + 2 * _nbytes((ka, tn), BF16) + 2 * _nbytes((tm, tn), F32)
    return pl.pallas_call(
        _mm2_res_kernel,
        grid=(m // tm, n // tn),
        in_specs=[pl.BlockSpec((tm, ka), lambda i, j: (i, 0)),
                  pl.BlockSpec((tm, kb), lambda i, j: (i, 0)),
                  pl.BlockSpec((None, ka, tn), lambda i, j: (layer, 0, j)),
                  pl.BlockSpec((None, kb, tn), lambda i, j: (layer, 1, j)),
                  pl.BlockSpec((tm, tn), lambda i, j: (i, j))],
        out_specs=pl.BlockSpec((tm, tn), lambda i, j: (i, j)),
        out_shape=jax.ShapeDtypeStruct((m, n), F32),
        compiler_params=_params(blk, 2 * _nbytes((tm, tn), F32)),
        name="matmul2_residual",
    )(xa, xb, w, w, res)


def _softplus(z):
    return jnp.maximum(z, 0.0) + jnp.log1p(jnp.exp(-jnp.abs(z)))


def _lru_kernel(u_ref, g_ref, cw_ref, cb_ref, wa_ref, ba_ref, wx_ref, bx_ref, lam_ref,
                o_ref, uc_ref, h_ref, *, chunk):
    t_len = u_ref.shape[0]
    n_chunks = t_len // chunk
    n_blk = chunk // 8
    cw = cw_ref[...]
    cb = cb_ref[...]
    row8 = lax.broadcasted_iota(jnp.int32, (8, HEAD_DIM), 0)

    def coeffs(uc, d):
        ucb = uc.astype(BF16)
        gate_r = _dot(ucb, wa_ref[d]) + ba_ref[d:d + 1, :]
        gate_i = _dot(ucb, wx_ref[d]) + bx_ref[d:d + 1, :]
        r = jax.nn.sigmoid(gate_r)
        i = jax.nn.sigmoid(gate_i)
        log_a = (-LRU_C * r) * _softplus(-lam_ref[d:d + 1, :])
        a = jnp.exp(log_a)
        th = jnp.tanh(log_a)
        b = jnp.sqrt(-2.0 * th / (1.0 - th)) * (i * uc)
        return a, b

    def tile_scan(a, b, reverse):
        for s in (1, 2, 4):
            if reverse:
                keep = row8 < 8 - s
                shift = 8 - s
            else:
                keep = row8 >= s
                shift = s
            a_sh = jnp.where(keep, pltpu.roll(a, shift, 0), 1.0)
            b_sh = jnp.where(keep, pltpu.roll(b, shift, 0), 0.0)
            b = a * b_sh + b
            a = a * a_sh
        return a, b

    def fwd_chunk(c, carry):
        c0 = pl.multiple_of(c * chunk, 8)
        main = u_ref[pl.ds(c0, chunk), :]
        prev = u_ref[pl.ds(jnp.maximum(c0 - 8, 0), 8), :]
        prev = jnp.where(c > 0, prev, 0.0)
        nxt = u_ref[pl.ds(jnp.minimum(c0 + chunk, t_len - 8), 8), :]
        nxt = jnp.where(c < n_chunks - 1, nxt, 0.0)
        x = jnp.concatenate([prev, main, nxt], axis=0)
        uc = (x[6:6 + chunk] * cw[0:1] + x[7:7 + chunk] * cw[1:2]
              + x[8:8 + chunk] * cw[2:3] + x[9:9 + chunk] * cw[3:4]) + cb
        uc_ref[pl.ds(c0, chunk), :] = uc
        a, b = coeffs(uc, 0)
        for k in range(n_blk):
            at, bt = tile_scan(a[8 * k:8 * k + 8], b[8 * k:8 * k + 8], False)
            ht = at * carry + bt
            h_ref[pl.ds(c0 + 8 * k, 8), :] = ht
            carry = jnp.broadcast_to(ht[7:8, :], (8, HEAD_DIM))
        return carry

    lax.fori_loop(0, n_chunks, fwd_chunk, jnp.zeros((8, HEAD_DIM), F32))

    def rev_chunk(ci, carry):
        c = n_chunks - 1 - ci
        c0 = pl.multiple_of(c * chunk, 8)
        uc = uc_ref[pl.ds(c0, chunk), :]
        a, b = coeffs(uc, 1)
        for k in reversed(range(n_blk)):
            at, bt = tile_scan(a[8 * k:8 * k + 8], b[8 * k:8 * k + 8], True)
            ht = at * carry + bt
            rows = pl.ds(c0 + 8 * k, 8)
            h_ref[rows, :] = h_ref[rows, :] + ht
            carry = jnp.broadcast_to(ht[0:1, :], (8, HEAD_DIM))
        rows = pl.ds(c0, chunk)
        o_ref[rows, :] = (h_ref[rows, :] * jax.nn.gelu(g_ref[rows, :])).astype(o_ref.dtype)
        return carry

    lax.fori_loop(0, n_chunks, rev_chunk, jnp.zeros((8, HEAD_DIM), F32))


def rglru(ug, conv_w, conv_b, w_a, b_a, w_x, b_x, lam, chunk=LRU_L):
    t_len = ug.shape[0]
    width = ug.shape[1] // 2
    nb = width // HEAD_DIM
    c = HEAD_DIM
    blk = (2 * _nbytes((t_len, c), F32) + _nbytes((t_len, c), BF16)
           + 4 * _nbytes((c, c), BF16))
    scratch = 2 * _nbytes((t_len, c), F32)
    return pl.pallas_call(
        functools.partial(_lru_kernel, chunk=chunk),
        grid=(nb,),
        in_specs=[pl.BlockSpec((t_len, c), lambda n: (0, n)),
                  pl.BlockSpec((t_len, c), lambda n: (0, n + nb)),
                  pl.BlockSpec((4, c), lambda n: (0, n)),
                  pl.BlockSpec((1, c), lambda n: (0, n)),
                  pl.BlockSpec((2, None, c, c), lambda n: (0, n, 0, 0)),
                  pl.BlockSpec((2, c), lambda n: (0, n)),
                  pl.BlockSpec((2, None, c, c), lambda n: (0, n, 0, 0)),
                  pl.BlockSpec((2, c), lambda n: (0, n)),
                  pl.BlockSpec((2, c), lambda n: (0, n))],
        out_specs=pl.BlockSpec((t_len, c), lambda n: (0, n)),
        out_shape=jax.ShapeDtypeStruct((t_len, width), BF16),
        scratch_shapes=[pltpu.VMEM((t_len, c), F32), pltpu.VMEM((t_len, c), F32)],
        compiler_params=_params(blk, scratch + 16 * _nbytes((chunk, c), F32)),
        name="rglru",
    )(ug, ug, conv_w, conv_b.reshape(1, width), w_a.astype(BF16), b_a, w_x.astype(BF16), b_x, lam)


_NT = (((1,), (1,)), ((), ()))


def _sink_softmax_pv(s, sink, v):
    m = jnp.maximum(jnp.max(s, axis=1, keepdims=True), sink)
    p = jnp.exp(s - m)
    den = jnp.sum(p, axis=1, keepdims=True) + jnp.exp(sink - m)
    return _dot((p / den).astype(BF16), v)


def _swa_kernel(sink_ref, slope_ref, q_ref, k_ref, v_ref, o_ref):
    h = pl.program_id(0)
    n_real = q_ref.shape[0] - N_META
    n_blocks = n_real // BLOCK
    d = HEAD_DIM
    zpad = jnp.zeros((BLOCK - N_META, d), BF16)
    k_meta = jnp.concatenate([k_ref[0:N_META, :], zpad], axis=0)
    v_meta = jnp.concatenate([v_ref[0:N_META, :], zpad], axis=0)
    kw = 3 * BLOCK
    col = lax.broadcasted_iota(jnp.int32, (BLOCK, BLOCK + kw), 1)
    rowq = lax.broadcasted_iota(jnp.int32, (BLOCK, BLOCK + kw), 0)

    def block(j, carry):
        r0 = pl.multiple_of(N_META + BLOCK * j, N_META)
        ws = jnp.clip(BLOCK * (j - 1), 0, n_real - kw)
        w0 = pl.multiple_of(N_META + ws, N_META)
        kcat = jnp.concatenate([k_meta, k_ref[pl.ds(w0, kw), :]], axis=0)
        vcat = jnp.concatenate([v_meta, v_ref[pl.ds(w0, kw), :]], axis=0)
        rel = jnp.abs((BLOCK * j + rowq) - (ws + col - BLOCK))
        in_win = (col >= BLOCK) & (rel <= WINDOW)
        relf = rel.astype(F32)
        qb = q_ref[pl.ds(r0, BLOCK), :]
        for g in range(GQA_GROUP):
            slope = slope_ref[GQA_GROUP * h + g]
            sink = sink_ref[GQA_GROUP * h + g]
            bias = jnp.where(col < N_META, 0.0, jnp.where(in_win, -slope * relf, NEG_INF))
            s = lax.dot_general(qb[:, g * d:(g + 1) * d], kcat, _NT,
                                preferred_element_type=F32) * ATTN_SCALE + bias
            o = _sink_softmax_pv(s, sink, vcat)
            o_ref[pl.ds(r0, BLOCK), g * d:(g + 1) * d] = o.astype(o_ref.dtype)
        return carry

    lax.fori_loop(0, n_blocks, block, 0)

    kcat = jnp.concatenate([k_meta, k_ref[N_META:N_META + BLOCK, :]], axis=0)
    vcat = jnp.concatenate([v_meta, v_ref[N_META:N_META + BLOCK, :]], axis=0)
    colm = lax.broadcasted_iota(jnp.int32, (N_META, 2 * BLOCK), 1)
    rowm = lax.broadcasted_iota(jnp.int32, (N_META, 2 * BLOCK), 0)
    ok = (colm < N_META) | ((colm >= BLOCK) & (colm - BLOCK + N_META - rowm <= WINDOW))
    bias = jnp.where(ok, 0.0, NEG_INF)
    qm = q_ref[0:N_META, :]
    for g in range(GQA_GROUP):
        sink = sink_ref[GQA_GROUP * h + g]
        s = lax.dot_general(qm[:, g * d:(g + 1) * d], kcat, _NT,
                            preferred_element_type=F32) * ATTN_SCALE + bias
        o = _sink_softmax_pv(s, sink, vcat)
        o_ref[0:N_META, g * d:(g + 1) * d] = o.astype(o_ref.dtype)


def windowed_attention(qkv, sink):
    t_len = qkv.shape[0]
    d = HEAD_DIM
    gw = GQA_GROUP * d
    slopes = jnp.asarray(2.0 ** (-8.0 * np.arange(1, SWA_Q_HEADS + 1) / SWA_Q_HEADS), F32)
    k_blk0 = SWA_Q_DIM // d
    v_blk0 = (SWA_Q_DIM + SWA_KV_DIM) // d
    blk = 2 * _nbytes((t_len, gw), BF16) + 2 * _nbytes((t_len, d), BF16)
    smem = pl.BlockSpec(memory_space=pltpu.SMEM)
    return pl.pallas_call(
        _swa_kernel,
        grid=(SWA_KV_HEADS,),
        in_specs=[smem, smem,
                  pl.BlockSpec((t_len, gw), lambda h: (0, h)),
                  pl.BlockSpec((t_len, d), lambda h: (0, k_blk0 + h)),
                  pl.BlockSpec((t_len, d), lambda h: (0, v_blk0 + h))],
        out_specs=pl.BlockSpec((t_len, gw), lambda h: (0, h)),
        out_shape=jax.ShapeDtypeStruct((t_len, SWA_Q_DIM), BF16),
        compiler_params=_params(blk, 8 * 2**20),
        name="windowed_attention",
    )(sink.astype(F32), slopes, qkv, qkv, qkv)


def _rope_tables(t_len, rows_pad):
    half = HEAD_DIM // 2
    n = t_len - N_META
    rows = n // GRID_W
    row = jnp.concatenate([jnp.zeros((N_META,), jnp.int32),
                           jnp.repeat(jnp.arange(rows, dtype=jnp.int32), GRID_W)])
    col = jnp.concatenate([jnp.zeros((N_META,), jnp.int32),
                           jnp.tile(jnp.arange(GRID_W, dtype=jnp.int32), rows)])
    inv_freq = ROPE_BASE ** (-(jnp.arange(half // 2, dtype=F32) * 2.0 / half))
    ang_row = row.astype(F32)[:, None] * inv_freq
    ang_col = col.astype(F32)[:, None] * inv_freq
    cos = jnp.concatenate([jnp.cos(ang_row), jnp.cos(ang_row), jnp.cos(ang_col), jnp.cos(ang_col)], axis=1)
    sin = jnp.concatenate([-jnp.sin(ang_row), jnp.sin(ang_row), -jnp.sin(ang_col), jnp.sin(ang_col)], axis=1)
    pad = ((0, rows_pad - t_len), (0, 0))
    return jnp.pad(cos, pad), jnp.pad(sin, pad)


def _qkv_prep_kernel(x_ref, cos_ref, sin_ref, qn_ref, kn_ref, q_ref, k_ref, v_ref, *, t_len):
    tm = x_ref.shape[0]
    d = HEAD_DIM
    cos = cos_ref[...]
    sin = sin_ref[...]
    lane = lax.broadcasted_iota(jnp.int32, (tm, d), 1)
    first = (lane % (d // 2)) < (d // 4)
    rows = pl.program_id(0) * tm + lax.broadcasted_iota(jnp.int32, (tm, 1), 0)
    valid = rows < t_len

    def norm_rope(xh, g):
        y = xh * lax.rsqrt(jnp.mean(xh * xh, axis=-1, keepdims=True) + RMS_EPS) * g
        partner = jnp.where(first, pltpu.roll(y, d - d // 4, 1), pltpu.roll(y, d // 4, 1))
        return y * cos + partner * sin

    qn = qn_ref[...]
    kn = kn_ref[...]
    for hd in range(GA_Q_HEADS):
        q_ref[:, hd * d:(hd + 1) * d] = norm_rope(x_ref[:, hd * d:(hd + 1) * d], qn).astype(q_ref.dtype)
    for hd in range(GA_KV_HEADS):
        kk = norm_rope(x_ref[:, GA_Q_DIM + hd * d:GA_Q_DIM + (hd + 1) * d], kn)
        k_ref[:, hd * d:(hd + 1) * d] = jnp.where(valid, kk, 0.0).astype(k_ref.dtype)
    v = x_ref[:, GA_Q_DIM + GA_KV_DIM:GA_Q_DIM + 2 * GA_KV_DIM]
    v_ref[...] = jnp.where(valid, v, 0.0).astype(v_ref.dtype)


def qkv_prep(qkv, q_norm, k_norm, cos, sin, tm=PREP_TM):
    t_len, width = qkv.shape
    d = HEAD_DIM
    blk = (_nbytes((tm, width), F32) + 2 * _nbytes((tm, d), F32)
           + _nbytes((tm, GA_Q_DIM), BF16) + 2 * _nbytes((tm, GA_KV_DIM), BF16))
    return pl.pallas_call(
        functools.partial(_qkv_prep_kernel, t_len=t_len),
        grid=(KEYS_PAD // tm,),
        in_specs=[pl.BlockSpec((tm, width), lambda i: (i, 0)),
                  pl.BlockSpec((tm, d), lambda i: (i, 0)),
                  pl.BlockSpec((tm, d), lambda i: (i, 0)),
                  pl.BlockSpec((1, d), lambda i: (0, 0)),
                  pl.BlockSpec((1, d), lambda i: (0, 0))],
        out_specs=[pl.BlockSpec((tm, GA_Q_DIM), lambda i: (i, 0)),
                   pl.BlockSpec((tm, GA_KV_DIM), lambda i: (i, 0)),
                   pl.BlockSpec((tm, GA_KV_DIM), lambda i: (i, 0))],
        out_shape=[jax.ShapeDtypeStruct((t_len, GA_Q_DIM), BF16),
                   jax.ShapeDtypeStruct((KEYS_PAD, GA_KV_DIM), BF16),
                   jax.ShapeDtypeStruct((KEYS_PAD, GA_KV_DIM), BF16)],
        compiler_params=_params(blk, 4 * 2**20),
        name="qkv_prep",
    )(qkv, cos, sin, q_norm.reshape(1, d), k_norm.reshape(1, d))


def _dense_attn_kernel(q_ref, k_ref, v_ref, o_ref, *, tk, n_keys):
    tq = q_ref.shape[0]
    d = HEAD_DIM
    rows = GQA_GROUP * tq
    n_chunks = k_ref.shape[0] // tk
    qs = jnp.concatenate([q_ref[:, g * d:(g + 1) * d] for g in range(GQA_GROUP)], axis=0)

    def chunk(c, carry, masked):
        m, l, acc = carry
        k0 = pl.multiple_of(c * tk, tk)
        s = lax.dot_general(qs, k_ref[pl.ds(k0, tk), :], _NT, preferred_element_type=F32) * ATTN_SCALE
        if masked:
            key = k0 + lax.broadcasted_iota(jnp.int32, (rows, tk), 1)
            s = jnp.where(key < n_keys, s, NEG_INF)
        m_new = jnp.maximum(m, jnp.max(s, axis=1, keepdims=True))
        alpha = jnp.exp(m - m_new)
        p = jnp.exp(s - m_new)
        l = alpha * l + jnp.sum(p, axis=1, keepdims=True)
        acc = alpha * acc + _dot(p.astype(BF16), v_ref[pl.ds(k0, tk), :])
        return m_new, l, acc

    init = (jnp.full((rows, 1), NEG_INF, F32), jnp.zeros((rows, 1), F32), jnp.zeros((rows, d), F32))
    carry = lax.fori_loop(0, n_chunks - 1, lambda c, cr: chunk(c, cr, False), init)
    _, l, acc = chunk(n_chunks - 1, carry, True)
    o = acc / l
    for g in range(GQA_GROUP):
        o_ref[:, g * d:(g + 1) * d] = o[g * tq:(g + 1) * tq].astype(o_ref.dtype)


def dense_attention(q, k, v, tq=DENSE_TQ, tk=DENSE_TK):
    t_len = q.shape[0]
    d = HEAD_DIM
    gw = GQA_GROUP * d
    kp = k.shape[0]
    blk = 2 * _nbytes((tq, gw), BF16) + 2 * _nbytes((kp, d), BF16)
    tmp = 6 * _nbytes((GQA_GROUP * tq, tk), F32)
    return pl.pallas_call(
        functools.partial(_dense_attn_kernel, tk=tk, n_keys=t_len),
        grid=(GA_KV_HEADS, t_len // tq),
        in_specs=[pl.BlockSpec((tq, gw), lambda h, i: (i, h)),
                  pl.BlockSpec((kp, d), lambda h, i: (0, h)),
                  pl.BlockSpec((kp, d), lambda h, i: (0, h))],
        out_specs=pl.BlockSpec((tq, gw), lambda h, i: (i, h)),
        out_shape=jax.ShapeDtypeStruct((t_len, GA_Q_DIM), BF16),
        compiler_params=_params(blk, tmp),
        name="dense_attention",
    )(q, k, v)


def _ffn(h, norm_g, w_gu, w_down, layer):
    hn = rmsnorm(h, norm_g[layer])
    act = matmul_swiglu(hn, w_gu, layer)
    return matmul_residual(act, w_down, layer, h, 0.5)


def _mixer_ab(h, hn, i, w_in, w_out, conv_w, conv_b, w_a, b_a, w_x, b_x, lam, sink):
    ug = matmul(hn, w_in, i, 0, 2 * LRU_WIDTH, F32, tn=1024)
    qkv = matmul(hn, w_in, i, 2 * LRU_WIDTH, SWA_Q_DIM + 2 * SWA_KV_DIM, BF16, tn=512)
    y_a = rglru(ug, conv_w[i], conv_b[i], w_a[i], b_a[i], w_x[i], b_x[i], lam[i])
    y_b = windowed_attention(qkv, sink[i])
    return matmul2_residual(y_a, y_b, w_out, i, h)


def _mixer_c(h, hn, i, w_in, w_out, q_norm, k_norm, cos, sin):
    qkv = matmul(hn, w_in, i, 0, GA_Q_DIM + 2 * GA_KV_DIM, F32, tn=1024)
    q, k, v = qkv_prep(qkv, q_norm[i], k_norm[i], cos, sin)
    o = dense_attention(q, k, v)
    return matmul_residual(o, w_out, i, h, 1.0)


def kernel(x, meta_tokens, ffn1_norm, ffn1_w_gu, ffn1_w_down, mix_norm, ab_w_in, ab_w_out, lru_conv_w, lru_conv_b, lru_w_a, lru_b_a, lru_w_x, lru_b_x, lru_lambda, swa_sink, c_w_in, c_w_out, c_q_norm, c_k_norm, ffn2_norm, ffn2_w_gu, ffn2_w_down, final_norm):
    bsz, n, d = x.shape
    assert (bsz, n, d) == (1, SEQ, D_MODEL)
    cos, sin = _rope_tables(T_TOK, KEYS_PAD)
    w_gu1, w_dn1 = ffn1_w_gu.astype(BF16), ffn1_w_down.astype(BF16)
    w_gu2, w_dn2 = ffn2_w_gu.astype(BF16), ffn2_w_down.astype(BF16)
    ab_in, ab_out = ab_w_in.astype(BF16), ab_w_out.astype(BF16)
    c_in, c_out = c_w_in.astype(BF16), c_w_out.astype(BF16)
    h = jnp.concatenate([meta_tokens.astype(x.dtype), x[0]], axis=0)
    for layer in range(DEPTH):
        h = _ffn(h, ffn1_norm, w_gu1, w_dn1, layer)
        hn = rmsnorm(h, mix_norm[layer])
        i = layer // 2
        if layer % 2 == 0:
            h = _mixer_ab(h, hn, i, ab_in, ab_out, lru_conv_w, lru_conv_b, lru_w_a, lru_b_a,
                          lru_w_x, lru_b_x, lru_lambda, swa_sink)
        else:
            h = _mixer_c(h, hn, i, c_in, c_out, c_q_norm, c_k_norm, cos, sin)
        h = _ffn(h, ffn2_norm, w_gu2, w_dn2, layer)
    return final_rmsnorm(h, final_norm)[None]
```

```python
import functools

import numpy as np
import jax
import jax.numpy as jnp
from jax import lax
from jax.experimental import pallas as pl
from jax.experimental.pallas import tpu as pltpu

F32 = jnp.float32
BF16 = jnp.bfloat16

D_MODEL = 4096
SEQ = 8192
DEPTH = 4
HEAD_DIM = 128
N_META = 16
T_TOK = SEQ + N_META
GRID_W = 64
BLOCK = 128
WINDOW = 128
RMS_EPS = 1e-6
NEG_INF = -1e30
D_FF = (3 * D_MODEL) // 2
LRU_WIDTH = D_MODEL // 2
LRU_C = 8.0
SWA_Q_HEADS = 16
SWA_KV_HEADS = 4
SWA_Q_DIM = SWA_Q_HEADS * HEAD_DIM
SWA_KV_DIM = SWA_KV_HEADS * HEAD_DIM
GA_Q_HEADS = 32
GA_KV_HEADS = 8
GA_Q_DIM = GA_Q_HEADS * HEAD_DIM
GA_KV_DIM = GA_KV_HEADS * HEAD_DIM
GQA_GROUP = 4
ROPE_BASE = 10000.0
ATTN_SCALE = HEAD_DIM ** -0.5

V7X_VMEM_BYTES = 64 * 2**20
VMEM_CAP_BYTES = V7X_VMEM_BYTES - 6 * 2**20

TM = 912
LANES = 128
MXU_DIM = 256
KEYS_PAD = 33 * MXU_DIM
DENSE_TK = 3 * MXU_DIM
DENSE_TQ = 912
PREP_TM = 384
NORM_TM = 432
LRU_L = 432
EXT = 2 * HEAD_DIM
EXP2_SCALE = ATTN_SCALE * float(np.log2(np.e))
PAD_KEY_PENALTY = 30000.0
FAST_SOFTMAX_MAX_LOGIT = 40.0


def _nbytes(shape, dtype):
    return int(np.prod(shape)) * jnp.dtype(dtype).itemsize


def _params(block_bytes, extra_bytes=0):
    need = 2 * block_bytes + extra_bytes + 4 * 2**20
    return pltpu.CompilerParams(vmem_limit_bytes=int(min(max(need, 16 * 2**20), VMEM_CAP_BYTES)))


def _norm_stats_kernel(x_ref, g_ref, xg_ref, rstd_ref):
    x = x_ref[...]
    xg_ref[...] = (x * g_ref[...]).astype(xg_ref.dtype)
    rstd = lax.rsqrt(jnp.mean(x * x, axis=-1, keepdims=True) + RMS_EPS)
    rstd_ref[...] = jnp.broadcast_to(rstd, rstd_ref.shape)


def norm_stats(x, g, tm=NORM_TM):
    m, d = x.shape
    blk = _nbytes((tm, d), F32) + _nbytes((tm, d), BF16) + _nbytes((tm, LANES), F32)
    return pl.pallas_call(
        _norm_stats_kernel,
        grid=(m // tm,),
        in_specs=[pl.BlockSpec((tm, d), lambda i: (i, 0)),
                  pl.BlockSpec((1, d), lambda i: (0, 0))],
        out_specs=[pl.BlockSpec((tm, d), lambda i: (i, 0)),
                   pl.BlockSpec((tm, LANES), lambda i: (i, 0))],
        out_shape=[jax.ShapeDtypeStruct((m, d), BF16),
                   jax.ShapeDtypeStruct((m, LANES), F32)],
        compiler_params=_params(blk, _nbytes((tm, d), F32)),
        name="norm_stats",
    )(x, g.reshape(1, d))


def _final_norm_kernel(a_ref, b_ref, g_ref, o_ref):
    x = jnp.concatenate([a_ref[N_META:, :], b_ref[:N_META, :]], axis=0)
    y = x * lax.rsqrt(jnp.mean(x * x, axis=-1, keepdims=True) + RMS_EPS)
    o_ref[...] = y * g_ref[...]


def final_rmsnorm(h, g, tm=256):
    m, d = h.shape
    n = m - N_META
    blk = 3 * _nbytes((tm, d), F32)
    return pl.pallas_call(
        _final_norm_kernel,
        grid=(n // tm,),
        in_specs=[pl.BlockSpec((tm, d), lambda i: (i, 0)),
                  pl.BlockSpec((tm, d), lambda i: (i + 1, 0)),
                  pl.BlockSpec((1, d), lambda i: (0, 0))],
        out_specs=pl.BlockSpec((tm, d), lambda i: (i, 0)),
        out_shape=jax.ShapeDtypeStruct((n, d), F32),
        compiler_params=_params(blk, 2 * _nbytes((tm, d), F32)),
        name="final_norm",
    )(h, h, g.reshape(1, d))


def _dot(a, b):
    return jnp.dot(a, b, preferred_element_type=F32)


def _lane_tile(v, width):
    return jnp.concatenate([v] * (width // LANES), axis=1)


def _mm_kernel(x_ref, rstd_ref, w_ref, o_ref):
    acc = _dot(x_ref[...], w_ref[...])
    o_ref[...] = (acc * _lane_tile(rstd_ref[...], acc.shape[1])).astype(o_ref.dtype)


def matmul(xg, rstd, w, layer, n_off, n, out_dtype, tn, tm=TM):
    m, k = xg.shape
    joff = n_off // tn
    blk = (_nbytes((tm, k), BF16) + _nbytes((tm, LANES), F32) + _nbytes((k, tn), BF16)
           + _nbytes((tm, tn), out_dtype))
    return pl.pallas_call(
        _mm_kernel,
        grid=(m // tm, n // tn),
        in_specs=[pl.BlockSpec((tm, k), lambda i, j: (i, 0)),
                  pl.BlockSpec((tm, LANES), lambda i, j: (i, 0)),
                  pl.BlockSpec((None, k, tn), lambda i, j: (layer, 0, j + joff))],
        out_specs=pl.BlockSpec((tm, tn), lambda i, j: (i, j)),
        out_shape=jax.ShapeDtypeStruct((m, n), out_dtype),
        compiler_params=_params(blk, 2 * _nbytes((tm, tn), F32)),
        name="matmul",
    )(xg, rstd, w)


def _mm_swiglu_kernel(x_ref, rstd_ref, wg_ref, wu_ref, o_ref):
    x = x_ref[...]
    rstd = _lane_tile(rstd_ref[...], o_ref.shape[1])
    gate = _dot(x, wg_ref[...]) * rstd
    up = _dot(x, wu_ref[...]) * rstd
    o_ref[...] = (jax.nn.silu(gate) * up).astype(o_ref.dtype)


def matmul_swiglu(xg, rstd, w_gu, layer, tn=512, tm=TM):
    m, k = xg.shape
    f = w_gu.shape[2] // 2
    uoff = f // tn
    blk = (_nbytes((tm, k), BF16) + _nbytes((tm, LANES), F32) + 2 * _nbytes((k, tn), BF16)
           + _nbytes((tm, tn), BF16))
    return pl.pallas_call(
        _mm_swiglu_kernel,
        grid=(m // tm, f // tn),
        in_specs=[pl.BlockSpec((tm, k), lambda i, j: (i, 0)),
                  pl.BlockSpec((tm, LANES), lambda i, j: (i, 0)),
                  pl.BlockSpec((None, k, tn), lambda i, j: (layer, 0, j)),
                  pl.BlockSpec((None, k, tn), lambda i, j: (layer, 0, j + uoff))],
        out_specs=pl.BlockSpec((tm, tn), lambda i, j: (i, j)),
        out_shape=jax.ShapeDtypeStruct((m, f), BF16),
        compiler_params=_params(blk, 3 * _nbytes((tm, tn), F32)),
        name="matmul_swiglu",
    )(xg, rstd, w_gu, w_gu)


def _mm_res_kernel(*refs, n_x, scale, with_stats, d_model):
    x_refs, w_refs = refs[:n_x], refs[n_x:2 * n_x]
    r_ref = refs[2 * n_x]
    acc = _dot(x_refs[0][...], w_refs[0][...])
    for x_ref, w_ref in zip(x_refs[1:], w_refs[1:]):
        acc = acc + _dot(x_ref[...], w_ref[...])
    h = r_ref[...] + scale * acc
    if not with_stats:
        o_ref, = refs[2 * n_x + 1:]
        o_ref[...] = h
        return
    g_ref, o_ref, xg_ref, rstd_ref, ssq_ref = refs[2 * n_x + 1:]
    o_ref[...] = h
    xg_ref[...] = (h * g_ref[...]).astype(xg_ref.dtype)
    sq = h * h
    part = sq[:, 0:LANES]
    for c in range(1, sq.shape[1] // LANES):
        part = part + sq[:, c * LANES:(c + 1) * LANES]
    j = pl.program_id(1)

    @pl.when(j == 0)
    def _():
        ssq_ref[...] = part

    @pl.when(j > 0)
    def _():
        ssq_ref[...] += part

    @pl.when(j == pl.num_programs(1) - 1)
    def _():
        tot = jnp.sum(ssq_ref[...], axis=-1, keepdims=True)
        rstd_ref[...] = jnp.broadcast_to(lax.rsqrt(tot * (1.0 / d_model) + RMS_EPS), rstd_ref.shape)


def matmul_residual(xs, w, layer, res, scale, g_next=None, tn=512, tm=TM):
    m, k = xs[0].shape
    n_x = len(xs)
    assert all(x.shape == (m, k) for x in xs)
    n = w.shape[2]
    with_stats = g_next is not None
    blk = (n_x * (_nbytes((tm, k), BF16) + _nbytes((k, tn), BF16)) + 2 * _nbytes((tm, tn), F32)
           + _nbytes((tm, tn), BF16) + _nbytes((tm, LANES), F32))
    in_specs = [pl.BlockSpec((tm, k), lambda i, j: (i, 0)) for _ in xs]
    in_specs += [pl.BlockSpec((None, k, tn), functools.partial(lambda i, j, c: (layer, c, j), c=c))
                 for c in range(n_x)]
    in_specs.append(pl.BlockSpec((tm, tn), lambda i, j: (i, j)))
    out_specs = [pl.BlockSpec((tm, tn), lambda i, j: (i, j))]
    out_shape = [jax.ShapeDtypeStruct((m, n), F32)]
    args = [*xs, *([w] * n_x), res]
    scratch = []
    if with_stats:
        in_specs.append(pl.BlockSpec((1, tn), lambda i, j: (0, j)))
        out_specs += [pl.BlockSpec((tm, tn), lambda i, j: (i, j)),
                      pl.BlockSpec((tm, LANES), lambda i, j: (i, 0))]
        out_shape += [jax.ShapeDtypeStruct((m, n), BF16), jax.ShapeDtypeStruct((m, LANES), F32)]
        args.append(g_next.reshape(1, n))
        scratch = [pltpu.VMEM((tm, LANES), F32)]
    out = pl.pallas_call(
        functools.partial(_mm_res_kernel, n_x=n_x, scale=scale, with_stats=with_stats, d_model=n),
        grid=(m // tm, n // tn),
        in_specs=in_specs,
        out_specs=out_specs,
        out_shape=out_shape,
        scratch_shapes=scratch,
        compiler_params=_params(blk, 3 * _nbytes((tm, tn), F32)),
        name="matmul_residual",
    )(*args)
    return out if with_stats else out[0]


def _softplus(z):
    return jnp.maximum(z, 0.0) + jnp.log1p(jnp.exp(-jnp.abs(z)))


def _lru_kernel(u_ref, g_ref, cw_ref, cb_ref, wa_ref, ba_ref, wx_ref, bx_ref, lam_ref,
                o_ref, uc_ref, h_ref, *, chunk):
    t_len = u_ref.shape[0]
    n_chunks = t_len // chunk
    n_blk = chunk // 8
    cw = cw_ref[...]
    cb = cb_ref[...]
    row8 = lax.broadcasted_iota(jnp.int32, (8, HEAD_DIM), 0)

    def coeffs(uc, d):
        ucb = uc.astype(BF16)
        gate_r = _dot(ucb, wa_ref[d]) + ba_ref[d:d + 1, :]
        gate_i = _dot(ucb, wx_ref[d]) + bx_ref[d:d + 1, :]
        r = jax.nn.sigmoid(gate_r)
        i = jax.nn.sigmoid(gate_i)
        log_a = (-LRU_C * r) * _softplus(-lam_ref[d:d + 1, :])
        a = jnp.exp(log_a)
        th = jnp.tanh(log_a)
        b = jnp.sqrt(-2.0 * th / (1.0 - th)) * (i * uc)
        return a, b

    def tile_scan(a, b, reverse):
        for s in (1, 2, 4):
            if reverse:
                keep = row8 < 8 - s
                shift = 8 - s
            else:
                keep = row8 >= s
                shift = s
            a_sh = jnp.where(keep, pltpu.roll(a, shift, 0), 1.0)
            b_sh = jnp.where(keep, pltpu.roll(b, shift, 0), 0.0)
            b = a * b_sh + b
            a = a * a_sh
        return a, b

    def fwd_chunk(c, carry):
        c0 = pl.multiple_of(c * chunk, 8)
        main = u_ref[pl.ds(c0, chunk), :]
        prev = u_ref[pl.ds(jnp.maximum(c0 - 8, 0), 8), :]
        prev = jnp.where(c > 0, prev, 0.0)
        nxt = u_ref[pl.ds(jnp.minimum(c0 + chunk, t_len - 8), 8), :]
        nxt = jnp.where(c < n_chunks - 1, nxt, 0.0)
        x = jnp.concatenate([prev, main, nxt], axis=0)
        uc = (x[6:6 + chunk] * cw[0:1] + x[7:7 + chunk] * cw[1:2]
              + x[8:8 + chunk] * cw[2:3] + x[9:9 + chunk] * cw[3:4]) + cb
        uc_ref[pl.ds(c0, chunk), :] = uc
        a, b = coeffs(uc, 0)
        for k in range(n_blk):
            at, bt = tile_scan(a[8 * k:8 * k + 8], b[8 * k:8 * k + 8], False)
            ht = at * carry + bt
            h_ref[pl.ds(c0 + 8 * k, 8), :] = ht
            carry = jnp.broadcast_to(ht[7:8, :], (8, HEAD_DIM))
        return carry

    lax.fori_loop(0, n_chunks, fwd_chunk, jnp.zeros((8, HEAD_DIM), F32))

    def rev_chunk(ci, carry):
        c = n_chunks - 1 - ci
        c0 = pl.multiple_of(c * chunk, 8)
        uc = uc_ref[pl.ds(c0, chunk), :]
        a, b = coeffs(uc, 1)
        for k in reversed(range(n_blk)):
            at, bt = tile_scan(a[8 * k:8 * k + 8], b[8 * k:8 * k + 8], True)
            ht = at * carry + bt
            rows = pl.ds(c0 + 8 * k, 8)
            h_ref[rows, :] = h_ref[rows, :] + ht
            carry = jnp.broadcast_to(ht[0:1, :], (8, HEAD_DIM))
        rows = pl.ds(c0, chunk)
        o_ref[rows, :] = (h_ref[rows, :] * jax.nn.gelu(g_ref[rows, :])).astype(o_ref.dtype)
        return carry

    lax.fori_loop(0, n_chunks, rev_chunk, jnp.zeros((8, HEAD_DIM), F32))


def rglru(ug, conv_w, conv_b, w_a, b_a, w_x, b_x, lam, chunk=LRU_L):
    t_len = ug.shape[0]
    width = ug.shape[1] // 2
    nb = width // HEAD_DIM
    c = HEAD_DIM
    blk = (2 * _nbytes((t_len, c), F32) + _nbytes((t_len, c), BF16)
           + 4 * _nbytes((c, c), BF16))
    scratch = 2 * _nbytes((t_len, c), F32)
    return pl.pallas_call(
        functools.partial(_lru_kernel, chunk=chunk),
        grid=(nb,),
        in_specs=[pl.BlockSpec((t_len, c), lambda n: (0, n)),
                  pl.BlockSpec((t_len, c), lambda n: (0, n + nb)),
                  pl.BlockSpec((4, c), lambda n: (0, n)),
                  pl.BlockSpec((1, c), lambda n: (0, n)),
                  pl.BlockSpec((2, None, c, c), lambda n: (0, n, 0, 0)),
                  pl.BlockSpec((2, c), lambda n: (0, n)),
                  pl.BlockSpec((2, None, c, c), lambda n: (0, n, 0, 0)),
                  pl.BlockSpec((2, c), lambda n: (0, n)),
                  pl.BlockSpec((2, c), lambda n: (0, n))],
        out_specs=pl.BlockSpec((t_len, c), lambda n: (0, n)),
        out_shape=jax.ShapeDtypeStruct((t_len, width), BF16),
        scratch_shapes=[pltpu.VMEM((t_len, c), F32), pltpu.VMEM((t_len, c), F32)],
        compiler_params=_params(blk, scratch + 16 * _nbytes((chunk, c), F32)),
        name="rglru",
    )(ug, ug, conv_w, conv_b.reshape(1, width), w_a.astype(BF16), b_a, w_x.astype(BF16), b_x, lam)


_NT = (((1,), (1,)), ((), ()))


def _sink_softmax_pv(s, sink, v):
    m = jnp.maximum(jnp.max(s, axis=1, keepdims=True), sink)
    p = jnp.exp(s - m)
    den = jnp.sum(p, axis=1, keepdims=True) + jnp.exp(sink - m)
    return _dot(p.astype(BF16), v) / den


def _swa_kernel(sink_ref, slope_ref, q_ref, k_ref, v_ref, o_ref):
    h = pl.program_id(0)
    n_real = q_ref.shape[0] - N_META
    n_blocks = n_real // BLOCK
    d = HEAD_DIM
    zpad = jnp.zeros((BLOCK - N_META, d), BF16)
    k_meta = jnp.concatenate([k_ref[0:N_META, :], zpad], axis=0)
    v_meta = jnp.concatenate([v_ref[0:N_META, :], zpad], axis=0)
    kw = 3 * BLOCK

    def per_head_rows(ref, rows):
        return jnp.concatenate([jnp.full((rows, 1), ref[GQA_GROUP * h + g], F32)
                                for g in range(GQA_GROUP)], axis=0)

    def stack_heads(x):
        return jnp.concatenate([x[:, g * d:(g + 1) * d] for g in range(GQA_GROUP)], axis=0)

    sink = per_head_rows(sink_ref, BLOCK)
    neg_slope = -per_head_rows(slope_ref, BLOCK)
    col = lax.broadcasted_iota(jnp.int32, (GQA_GROUP * BLOCK, BLOCK + kw), 1)
    rowq = lax.broadcasted_iota(jnp.int32, (GQA_GROUP * BLOCK, BLOCK + kw), 0) % BLOCK
    is_meta = col < N_META
    in_cols = col >= BLOCK
    col_off = col - BLOCK - rowq

    def bias_for(first_key_minus_first_query):
        rel = jnp.abs(col_off + first_key_minus_first_query)
        in_win = in_cols & (rel <= WINDOW)
        return jnp.where(is_meta, 0.0, jnp.where(in_win, neg_slope * rel.astype(F32), NEG_INF))

    def attend(j, ws, bias):
        r0, w0 = N_META + BLOCK * j, N_META + ws
        if not isinstance(j, int):
            r0, w0 = pl.multiple_of(r0, N_META), pl.multiple_of(w0, N_META)
        kcat = jnp.concatenate([k_meta, k_ref[pl.ds(w0, kw), :]], axis=0)
        vcat = jnp.concatenate([v_meta, v_ref[pl.ds(w0, kw), :]], axis=0)
        qs = stack_heads(q_ref[pl.ds(r0, BLOCK), :])
        s = lax.dot_general(qs, kcat, _NT, preferred_element_type=F32) * ATTN_SCALE + bias
        o = _sink_softmax_pv(s, sink, vcat)
        for g in range(GQA_GROUP):
            o_ref[pl.ds(r0, BLOCK), g * d:(g + 1) * d] = o[g * BLOCK:(g + 1) * BLOCK].astype(o_ref.dtype)

    attend(0, 0, bias_for(0))
    bias_mid = bias_for(-BLOCK)

    def block(j, carry):
        attend(j, BLOCK * (j - 1), bias_mid)
        return carry

    lax.fori_loop(1, n_blocks - 1, block, 0, unroll=2)
    attend(n_blocks - 1, n_real - kw, bias_for(n_real - kw - BLOCK * (n_blocks - 1)))

    kcat = jnp.concatenate([k_meta, k_ref[N_META:N_META + BLOCK, :]], axis=0)
    vcat = jnp.concatenate([v_meta, v_ref[N_META:N_META + BLOCK, :]], axis=0)
    colm = lax.broadcasted_iota(jnp.int32, (GQA_GROUP * N_META, 2 * BLOCK), 1)
    rowm = lax.broadcasted_iota(jnp.int32, (GQA_GROUP * N_META, 2 * BLOCK), 0) % N_META
    ok = (colm < N_META) | ((colm >= BLOCK) & (colm - BLOCK + N_META - rowm <= WINDOW))
    bias = jnp.where(ok, 0.0, NEG_INF)
    qs = stack_heads(q_ref[0:N_META, :])
    s = lax.dot_general(qs, kcat, _NT, preferred_element_type=F32) * ATTN_SCALE + bias
    o = _sink_softmax_pv(s, per_head_rows(sink_ref, N_META), vcat)
    for g in range(GQA_GROUP):
        o_ref[0:N_META, g * d:(g + 1) * d] = o[g * N_META:(g + 1) * N_META].astype(o_ref.dtype)


def windowed_attention(qkv, sink):
    t_len = qkv.shape[0]
    d = HEAD_DIM
    gw = GQA_GROUP * d
    slopes = jnp.asarray(2.0 ** (-8.0 * np.arange(1, SWA_Q_HEADS + 1) / SWA_Q_HEADS), F32)
    k_blk0 = SWA_Q_DIM // d
    v_blk0 = (SWA_Q_DIM + SWA_KV_DIM) // d
    blk = 2 * _nbytes((t_len, gw), BF16) + 2 * _nbytes((t_len, d), BF16)
    smem = pl.BlockSpec(memory_space=pltpu.SMEM)
    return pl.pallas_call(
        _swa_kernel,
        grid=(SWA_KV_HEADS,),
        in_specs=[smem, smem,
                  pl.BlockSpec((t_len, gw), lambda h: (0, h)),
                  pl.BlockSpec((t_len, d), lambda h: (0, k_blk0 + h)),
                  pl.BlockSpec((t_len, d), lambda h: (0, v_blk0 + h))],
        out_specs=pl.BlockSpec((t_len, gw), lambda h: (0, h)),
        out_shape=jax.ShapeDtypeStruct((t_len, SWA_Q_DIM), BF16),
        compiler_params=_params(blk, 8 * 2**20),
        name="windowed_attention",
    )(sink.astype(F32), slopes, qkv, qkv, qkv)


def _rope_tables(t_len, rows_pad):
    half = HEAD_DIM // 2
    n = t_len - N_META
    rows = n // GRID_W
    row = jnp.concatenate([jnp.zeros((N_META,), jnp.int32),
                           jnp.repeat(jnp.arange(rows, dtype=jnp.int32), GRID_W)])
    col = jnp.concatenate([jnp.zeros((N_META,), jnp.int32),
                           jnp.tile(jnp.arange(GRID_W, dtype=jnp.int32), rows)])
    inv_freq = ROPE_BASE ** (-(jnp.arange(half // 2, dtype=F32) * 2.0 / half))
    ang_row = row.astype(F32)[:, None] * inv_freq
    ang_col = col.astype(F32)[:, None] * inv_freq
    cos = jnp.concatenate([jnp.cos(ang_row), jnp.cos(ang_row), jnp.cos(ang_col), jnp.cos(ang_col)], axis=1)
    sin = jnp.concatenate([-jnp.sin(ang_row), jnp.sin(ang_row), -jnp.sin(ang_col), jnp.sin(ang_col)], axis=1)
    pad = ((0, rows_pad - t_len), (0, 0))
    return jnp.pad(cos, pad), jnp.pad(sin, pad)


def _qkv_prep_kernel(x_ref, cos_ref, sin_ref, qn_ref, kn_ref, q_ref, k_ref, v_ref, *, t_len):
    tm = x_ref.shape[0]
    d = HEAD_DIM
    cos = cos_ref[...]
    sin = sin_ref[...]
    lane = lax.broadcasted_iota(jnp.int32, (tm, d), 1)
    first = (lane % (d // 2)) < (d // 4)
    rows = pl.program_id(0) * tm + lax.broadcasted_iota(jnp.int32, (tm, 1), 0)
    valid = rows < t_len

    def norm_rope(xh, g):
        y = xh * lax.rsqrt(jnp.mean(xh * xh, axis=-1, keepdims=True) + RMS_EPS) * g
        partner = jnp.where(first, pltpu.roll(y, d - d // 4, 1), pltpu.roll(y, d // 4, 1))
        return y * cos + partner * sin

    qn = qn_ref[...]
    kn = kn_ref[...]
    k_norm_bound = (d ** 0.5) * jnp.max(jnp.abs(kn), axis=-1, keepdims=True)
    for hd in range(GA_Q_HEADS):
        qq = norm_rope(x_ref[:, hd * d:(hd + 1) * d], qn)
        shift = jnp.sqrt(jnp.sum(qq * qq, axis=-1, keepdims=True)) * k_norm_bound
        aux = jnp.where(lane == 0, -shift, jnp.where(lane == 1, -PAD_KEY_PENALTY, 0.0))
        q_ref[:, hd * EXT:hd * EXT + d] = qq.astype(q_ref.dtype)
        q_ref[:, hd * EXT + d:(hd + 1) * EXT] = aux.astype(q_ref.dtype)
    k_aux = jnp.where(lane == 0, 1.0, jnp.where((lane == 1) & jnp.logical_not(valid), 1.0, 0.0))
    ones = jnp.ones((tm, d), v_ref.dtype)
    for hd in range(GA_KV_HEADS):
        kk = norm_rope(x_ref[:, GA_Q_DIM + hd * d:GA_Q_DIM + (hd + 1) * d], kn)
        k_ref[:, hd * EXT:hd * EXT + d] = jnp.where(valid, kk, 0.0).astype(k_ref.dtype)
        k_ref[:, hd * EXT + d:(hd + 1) * EXT] = k_aux.astype(k_ref.dtype)
        v = x_ref[:, GA_Q_DIM + GA_KV_DIM + hd * d:GA_Q_DIM + GA_KV_DIM + (hd + 1) * d]
        v_ref[:, hd * EXT:hd * EXT + d] = jnp.where(valid, v, 0.0).astype(v_ref.dtype)
        v_ref[:, hd * EXT + d:(hd + 1) * EXT] = ones


def qkv_prep(qkv, q_norm, k_norm, cos, sin, keys_pad=KEYS_PAD, tm=PREP_TM):
    t_len, width = qkv.shape
    d = HEAD_DIM
    blk = (_nbytes((tm, width), F32) + 2 * _nbytes((tm, d), F32)
           + _nbytes((tm, GA_Q_HEADS * EXT), BF16) + 2 * _nbytes((tm, GA_KV_HEADS * EXT), BF16))
    return pl.pallas_call(
        functools.partial(_qkv_prep_kernel, t_len=t_len),
        grid=(keys_pad // tm,),
        in_specs=[pl.BlockSpec((tm, width), lambda i: (i, 0)),
                  pl.BlockSpec((tm, d), lambda i: (i, 0)),
                  pl.BlockSpec((tm, d), lambda i: (i, 0)),
                  pl.BlockSpec((1, d), lambda i: (0, 0)),
                  pl.BlockSpec((1, d), lambda i: (0, 0))],
        out_specs=[pl.BlockSpec((tm, GA_Q_HEADS * EXT), lambda i: (i, 0)),
                   pl.BlockSpec((tm, GA_KV_HEADS * EXT), lambda i: (i, 0)),
                   pl.BlockSpec((tm, GA_KV_HEADS * EXT), lambda i: (i, 0))],
        out_shape=[jax.ShapeDtypeStruct((t_len, GA_Q_HEADS * EXT), BF16),
                   jax.ShapeDtypeStruct((keys_pad, GA_KV_HEADS * EXT), BF16),
                   jax.ShapeDtypeStruct((keys_pad, GA_KV_HEADS * EXT), BF16)],
        compiler_params=_params(blk, 4 * 2**20),
        name="qkv_prep",
    )(qkv, cos, sin, q_norm.reshape(1, d), k_norm.reshape(1, d))


def _dense_fast_kernel(q_ref, k_ref, v_ref, o_ref, *, tk):
    d = HEAD_DIM
    q = q_ref[...]
    acc = None
    for c in range(k_ref.shape[0] // tk):
        s = lax.dot_general(q, k_ref[c * tk:(c + 1) * tk, :], _NT, preferred_element_type=F32)
        p = jnp.exp2(s * EXP2_SCALE).astype(BF16)
        pv = _dot(p, v_ref[c * tk:(c + 1) * tk, :])
        acc = pv if acc is None else acc + pv
    o_ref[...] = (acc[:, :d] / acc[:, d:]).astype(o_ref.dtype)


def _dense_safe_kernel(q_ref, k_ref, v_ref, o_ref, *, tk, n_keys):
    d = HEAD_DIM
    q = q_ref[...]
    rows = q.shape[0]

    def chunk(c, carry):
        m, acc = carry
        k0 = pl.multiple_of(c * tk, tk)
        t = lax.dot_general(q, k_ref[pl.ds(k0, tk), :], _NT, preferred_element_type=F32) * EXP2_SCALE
        key = k0 + lax.broadcasted_iota(jnp.int32, (rows, tk), 1)
        t = jnp.where(key < n_keys, t, NEG_INF)
        m_new = jnp.maximum(m, jnp.max(t, axis=1, keepdims=True))
        p = jnp.exp2(t - m_new).astype(BF16)
        acc = jnp.exp2(m - m_new) * acc + _dot(p, v_ref[pl.ds(k0, tk), :])
        return m_new, acc

    init = (jnp.full((rows, 1), NEG_INF, F32), jnp.zeros((rows, EXT), F32))
    _, acc = lax.fori_loop(0, k_ref.shape[0] // tk, chunk, init)
    o_ref[...] = (acc[:, :d] / acc[:, d:]).astype(o_ref.dtype)


def dense_attention(q_ext, k_ext, v_ext, fast, tq=DENSE_TQ, tk=DENSE_TK):
    t_len = q_ext.shape[0]
    d = HEAD_DIM
    kp = k_ext.shape[0]
    blk = _nbytes((tq, EXT), BF16) + _nbytes((tq, d), BF16) + 2 * _nbytes((kp, EXT), BF16)
    if fast:
        body = functools.partial(_dense_fast_kernel, tk=tk)
    else:
        body = functools.partial(_dense_safe_kernel, tk=tk, n_keys=t_len)
    return pl.pallas_call(
        body,
        grid=(GA_Q_HEADS, t_len // tq),
        in_specs=[pl.BlockSpec((tq, EXT), lambda h, i: (i, h)),
                  pl.BlockSpec((kp, EXT), lambda h, i: (0, h // GQA_GROUP)),
                  pl.BlockSpec((kp, EXT), lambda h, i: (0, h // GQA_GROUP))],
        out_specs=pl.BlockSpec((tq, d), lambda h, i: (i, h)),
        out_shape=jax.ShapeDtypeStruct((t_len, GA_Q_DIM), BF16),
        compiler_params=_params(blk, 8 * _nbytes((tq, tk), F32)),
        name="dense_attention_fast" if fast else "dense_attention_safe",
    )(q_ext, k_ext, v_ext)


def _mixer_ab(xg, rstd, i, w_in, conv_w, conv_b, w_a, b_a, w_x, b_x, lam, sink):
    ug = matmul(xg, rstd, w_in, i, 0, 2 * LRU_WIDTH, F32, tn=1024)
    qkv = matmul(xg, rstd, w_in, i, 2 * LRU_WIDTH, SWA_Q_DIM + 2 * SWA_KV_DIM, BF16, tn=512)
    y_a = rglru(ug, conv_w[i], conv_b[i], w_a[i], b_a[i], w_x[i], b_x[i], lam[i])
    y_b = windowed_attention(qkv, sink[i])
    return [y_a, y_b]


def _mixer_c(xg, rstd, i, w_in, q_norm, k_norm, cos, sin):
    qkv = matmul(xg, rstd, w_in, i, 0, GA_Q_DIM + 2 * GA_KV_DIM, F32, tn=1024)
    ext = qkv_prep(qkv, q_norm[i], k_norm[i], cos, sin)
    logit_bound = ATTN_SCALE * HEAD_DIM * jnp.max(jnp.abs(q_norm[i])) * jnp.max(jnp.abs(k_norm[i]))
    o = lax.cond(logit_bound <= FAST_SOFTMAX_MAX_LOGIT,
                 lambda a: dense_attention(*a, fast=True),
                 lambda a: dense_attention(*a, fast=False), ext)
    return [o]


def kernel(x, meta_tokens, ffn1_norm, ffn1_w_gu, ffn1_w_down, mix_norm, ab_w_in, ab_w_out, lru_conv_w, lru_conv_b, lru_w_a, lru_b_a, lru_w_x, lru_b_x, lru_lambda, swa_sink, c_w_in, c_w_out, c_q_norm, c_k_norm, ffn2_norm, ffn2_w_gu, ffn2_w_down, final_norm):
    bsz, n, d = x.shape
    assert (bsz, n, d) == (1, SEQ, D_MODEL)
    cos, sin = _rope_tables(T_TOK, KEYS_PAD)
    w_gu1, w_dn1 = ffn1_w_gu.astype(BF16), ffn1_w_down.astype(BF16)
    w_gu2, w_dn2 = ffn2_w_gu.astype(BF16), ffn2_w_down.astype(BF16)
    ab_in, ab_out = ab_w_in.astype(BF16), ab_w_out.astype(BF16)
    c_in, c_out = c_w_in.astype(BF16), c_w_out.astype(BF16)
    h = jnp.concatenate([meta_tokens.astype(x.dtype), x[0]], axis=0)
    xg, rstd = norm_stats(h, ffn1_norm[0])
    for layer in range(DEPTH):
        act = matmul_swiglu(xg, rstd, w_gu1, layer)
        h, xg, rstd = matmul_residual([act], w_dn1, layer, h, 0.5, g_next=mix_norm[layer])
        i = layer // 2
        if layer % 2 == 0:
            ys = _mixer_ab(xg, rstd, i, ab_in, lru_conv_w, lru_conv_b, lru_w_a, lru_b_a,
                           lru_w_x, lru_b_x, lru_lambda, swa_sink)
            w_out = ab_out
        else:
            ys = _mixer_c(xg, rstd, i, c_in, c_q_norm, c_k_norm, cos, sin)
            w_out = c_out
        h, xg, rstd = matmul_residual(ys, w_out, i, h, 1.0, g_next=ffn2_norm[layer])
        act = matmul_swiglu(xg, rstd, w_gu2, layer)
        if layer + 1 < DEPTH:
            h, xg, rstd = matmul_residual([act], w_dn2, layer, h, 0.5, g_next=ffn1_norm[layer + 1])
        else:
            h = matmul_residual([act], w_dn2, layer, h, 0.5)
    return final_rmsnorm(h, final_norm)[None]
```

```python
import functools

import numpy as np
import jax
import jax.numpy as jnp
from jax import lax
from jax.experimental import pallas as pl
from jax.experimental.pallas import tpu as pltpu

F32 = jnp.float32
BF16 = jnp.bfloat16

D_MODEL = 4096
SEQ = 8192
DEPTH = 4
HEAD_DIM = 128
N_META = 16
T_TOK = SEQ + N_META
GRID_W = 64
BLOCK = 128
WINDOW = 128
RMS_EPS = 1e-6
NEG_INF = -1e30
D_FF = (3 * D_MODEL) // 2
LRU_WIDTH = D_MODEL // 2
LRU_C = 8.0
SWA_Q_HEADS = 16
SWA_KV_HEADS = 4
SWA_Q_DIM = SWA_Q_HEADS * HEAD_DIM
SWA_KV_DIM = SWA_KV_HEADS * HEAD_DIM
GA_Q_HEADS = 32
GA_KV_HEADS = 8
GA_Q_DIM = GA_Q_HEADS * HEAD_DIM
GA_KV_DIM = GA_KV_HEADS * HEAD_DIM
GQA_GROUP = 4
AB_IN = 2 * LRU_WIDTH + SWA_Q_DIM + 2 * SWA_KV_DIM
C_IN = GA_Q_DIM + 2 * GA_KV_DIM
ROPE_BASE = 10000.0
ATTN_SCALE = HEAD_DIM ** -0.5

V7X_VMEM_BYTES = 64 * 2**20
VMEM_CAP_BYTES = V7X_VMEM_BYTES - 6 * 2**20

TM = 912
LANES = 128
MXU_DIM = 256
KEYS_PAD = 33 * MXU_DIM
DENSE_TK = MXU_DIM
DENSE_TQ = 912
PREP_TM = 384
NORM_TM = 432
LRU_L = 432
EXT = 2 * HEAD_DIM
EXP2_SCALE = ATTN_SCALE * float(np.log2(np.e))
PAD_KEY_PENALTY = 30000.0
FAST_SOFTMAX_MAX_LOGIT = 40.0


def _nbytes(shape, dtype):
    return int(np.prod(shape)) * jnp.dtype(dtype).itemsize


def _params(block_bytes, extra_bytes=0):
    need = 2 * block_bytes + extra_bytes + 4 * 2**20
    return pltpu.CompilerParams(vmem_limit_bytes=int(min(max(need, 16 * 2**20), VMEM_CAP_BYTES)))


def _norm_stats_kernel(x_ref, g_ref, xg_ref, rstd_ref):
    x = x_ref[...]
    xg_ref[...] = (x * g_ref[...]).astype(xg_ref.dtype)
    rstd = lax.rsqrt(jnp.mean(x * x, axis=-1, keepdims=True) + RMS_EPS)
    rstd_ref[...] = jnp.broadcast_to(rstd, rstd_ref.shape)


def norm_stats(x, g, tm=NORM_TM):
    m, d = x.shape
    blk = _nbytes((tm, d), F32) + _nbytes((tm, d), BF16) + _nbytes((tm, LANES), F32)
    return pl.pallas_call(
        _norm_stats_kernel,
        grid=(m // tm,),
        in_specs=[pl.BlockSpec((tm, d), lambda i: (i, 0)),
                  pl.BlockSpec((1, d), lambda i: (0, 0))],
        out_specs=[pl.BlockSpec((tm, d), lambda i: (i, 0)),
                   pl.BlockSpec((tm, LANES), lambda i: (i, 0))],
        out_shape=[jax.ShapeDtypeStruct((m, d), BF16),
                   jax.ShapeDtypeStruct((m, LANES), F32)],
        compiler_params=_params(blk, _nbytes((tm, d), F32)),
        name="norm_stats",
    )(x, g.reshape(1, d))


def _final_norm_kernel(a_ref, b_ref, g_ref, o_ref):
    x = jnp.concatenate([a_ref[N_META:, :], b_ref[:N_META, :]], axis=0)
    y = x * lax.rsqrt(jnp.mean(x * x, axis=-1, keepdims=True) + RMS_EPS)
    o_ref[...] = y * g_ref[...]


def final_rmsnorm(h, g, tm=256):
    m, d = h.shape
    n = m - N_META
    blk = 3 * _nbytes((tm, d), F32)
    return pl.pallas_call(
        _final_norm_kernel,
        grid=(n // tm,),
        in_specs=[pl.BlockSpec((tm, d), lambda i: (i, 0)),
                  pl.BlockSpec((tm, d), lambda i: (i + 1, 0)),
                  pl.BlockSpec((1, d), lambda i: (0, 0))],
        out_specs=pl.BlockSpec((tm, d), lambda i: (i, 0)),
        out_shape=jax.ShapeDtypeStruct((n, d), F32),
        compiler_params=_params(blk, 2 * _nbytes((tm, d), F32)),
        name="final_norm",
    )(h, h, g.reshape(1, d))


def _dot(a, b):
    return jnp.dot(a, b, preferred_element_type=F32)


def _lane_tile(v, width):
    return jnp.concatenate([v] * (width // LANES), axis=1)


def _cast_specs(jobs, n_col_tiles, steps):
    in_specs, out_specs, out_shape, args, nbytes = [], [], [], [], 0
    for w, layer, rb, cb in jobs:
        _, k, n = w.shape
        assert k % rb == 0 and n % cb == 0 and (k // rb) * (n // cb) <= steps
        nc = n // cb
        last = (k // rb) * nc - 1

        def slab(i, j, nc=nc, last=last):
            t = jnp.minimum(i * n_col_tiles + j, last)
            return t // nc, t % nc

        in_specs.append(pl.BlockSpec((None, rb, cb),
                                     functools.partial(lambda i, j, f, l: (l, *f(i, j)), f=slab, l=layer)))
        out_specs.append(pl.BlockSpec((rb, cb), slab))
        out_shape.append(jax.ShapeDtypeStruct((k, n), BF16))
        args.append(w)
        nbytes += _nbytes((rb, cb), F32) + _nbytes((rb, cb), BF16)
    return in_specs, out_specs, out_shape, args, nbytes


def _run_cast_jobs(src_refs, dst_refs):
    for src, dst in zip(src_refs, dst_refs):
        dst[...] = src[...].astype(dst.dtype)


def _mm_kernel(*refs, n_cast):
    x_ref, rstd_ref, w_ref = refs[:3]
    o_ref = refs[3 + n_cast]
    _run_cast_jobs(refs[3:3 + n_cast], refs[4 + n_cast:])
    acc = _dot(x_ref[...], w_ref[...])
    o_ref[...] = (acc * _lane_tile(rstd_ref[...], acc.shape[1])).astype(o_ref.dtype)


def matmul(xg, rstd, w, n_off, n, out_dtype, tn, cast_jobs=(), tm=TM):
    m, k = xg.shape
    joff = n_off // tn
    grid = (m // tm, n // tn)
    c_in, c_out, c_shape, c_args, c_bytes = _cast_specs(cast_jobs, grid[1], grid[0] * grid[1])
    blk = (_nbytes((tm, k), BF16) + _nbytes((tm, LANES), F32) + _nbytes((k, tn), BF16)
           + _nbytes((tm, tn), out_dtype) + c_bytes)
    out = pl.pallas_call(
        functools.partial(_mm_kernel, n_cast=len(cast_jobs)),
        grid=grid,
        in_specs=[pl.BlockSpec((tm, k), lambda i, j: (i, 0)),
                  pl.BlockSpec((tm, LANES), lambda i, j: (i, 0)),
                  pl.BlockSpec((k, tn), lambda i, j: (0, j + joff)), *c_in],
        out_specs=[pl.BlockSpec((tm, tn), lambda i, j: (i, j)), *c_out],
        out_shape=[jax.ShapeDtypeStruct((m, n), out_dtype), *c_shape],
        compiler_params=_params(blk, 2 * _nbytes((tm, tn), F32)),
        name="matmul",
    )(xg, rstd, w, *c_args)
    return out[0], out[1:]


def _mm_swiglu_kernel(*refs, n_cast):
    x_ref, rstd_ref, wg_ref, wu_ref = refs[:4]
    o_ref = refs[4 + n_cast]
    _run_cast_jobs(refs[4:4 + n_cast], refs[5 + n_cast:])
    x = x_ref[...]
    rstd = _lane_tile(rstd_ref[...], o_ref.shape[1])
    gate = _dot(x, wg_ref[...]) * rstd
    up = _dot(x, wu_ref[...]) * rstd
    o_ref[...] = (jax.nn.silu(gate) * up).astype(o_ref.dtype)


def matmul_swiglu(xg, rstd, w_gu, cast_jobs=(), tn=512, tm=TM):
    m, k = xg.shape
    f = w_gu.shape[1] // 2
    uoff = f // tn
    grid = (m // tm, f // tn)
    c_in, c_out, c_shape, c_args, c_bytes = _cast_specs(cast_jobs, grid[1], grid[0] * grid[1])
    blk = (_nbytes((tm, k), BF16) + _nbytes((tm, LANES), F32) + 2 * _nbytes((k, tn), BF16)
           + _nbytes((tm, tn), BF16) + c_bytes)
    out = pl.pallas_call(
        functools.partial(_mm_swiglu_kernel, n_cast=len(cast_jobs)),
        grid=grid,
        in_specs=[pl.BlockSpec((tm, k), lambda i, j: (i, 0)),
                  pl.BlockSpec((tm, LANES), lambda i, j: (i, 0)),
                  pl.BlockSpec((k, tn), lambda i, j: (0, j)),
                  pl.BlockSpec((k, tn), lambda i, j: (0, j + uoff)), *c_in],
        out_specs=[pl.BlockSpec((tm, tn), lambda i, j: (i, j)), *c_out],
        out_shape=[jax.ShapeDtypeStruct((m, f), BF16), *c_shape],
        compiler_params=_params(blk, 3 * _nbytes((tm, tn), F32)),
        name="matmul_swiglu",
    )(xg, rstd, w_gu, w_gu, *c_args)
    return out[0], out[1:]


def _mm_res_kernel(*refs, n_x, n_cast, scale, with_stats, d_model):
    x_refs, w_refs = refs[:n_x], refs[n_x:2 * n_x]
    r_ref = refs[2 * n_x]
    n_in = 2 * n_x + 1 + int(with_stats)
    n_out = 3 if with_stats else 1
    outs = refs[n_in + n_cast:n_in + n_cast + n_out]
    _run_cast_jobs(refs[n_in:n_in + n_cast], refs[n_in + n_cast + n_out:n_in + 2 * n_cast + n_out])
    acc = _dot(x_refs[0][...], w_refs[0][...])
    for x_ref, w_ref in zip(x_refs[1:], w_refs[1:]):
        acc = acc + _dot(x_ref[...], w_ref[...])
    h = r_ref[...] + scale * acc
    if not with_stats:
        outs[0][...] = h
        return
    g_ref, ssq_ref = refs[n_in - 1], refs[-1]
    o_ref, xg_ref, rstd_ref = outs
    o_ref[...] = h
    xg_ref[...] = (h * g_ref[...]).astype(xg_ref.dtype)
    sq = h * h
    part = sq[:, 0:LANES]
    for c in range(1, sq.shape[1] // LANES):
        part = part + sq[:, c * LANES:(c + 1) * LANES]
    j = pl.program_id(1)

    @pl.when(j == 0)
    def _():
        ssq_ref[...] = part

    @pl.when(j > 0)
    def _():
        ssq_ref[...] += part

    @pl.when(j == pl.num_programs(1) - 1)
    def _():
        tot = jnp.sum(ssq_ref[...], axis=-1, keepdims=True)
        rstd_ref[...] = jnp.broadcast_to(lax.rsqrt(tot * (1.0 / d_model) + RMS_EPS), rstd_ref.shape)


def matmul_residual(xs, w, res, scale, g_next=None, cast_jobs=(), tn=512, tm=TM):
    m, k = xs[0].shape
    n_x = len(xs)
    assert all(x.shape == (m, k) for x in xs)
    n = w.shape[1]
    with_stats = g_next is not None
    grid = (m // tm, n // tn)
    c_in, c_out, c_shape, c_args, c_bytes = _cast_specs(cast_jobs, grid[1], grid[0] * grid[1])
    blk = (n_x * (_nbytes((tm, k), BF16) + _nbytes((k, tn), BF16)) + 2 * _nbytes((tm, tn), F32)
           + _nbytes((tm, tn), BF16) + _nbytes((tm, LANES), F32) + c_bytes)
    in_specs = [pl.BlockSpec((tm, k), lambda i, j: (i, 0)) for _ in xs]
    in_specs += [pl.BlockSpec((k, tn), functools.partial(lambda i, j, c: (c, j), c=c))
                 for c in range(n_x)]
    in_specs.append(pl.BlockSpec((tm, tn), lambda i, j: (i, j)))
    out_specs = [pl.BlockSpec((tm, tn), lambda i, j: (i, j))]
    out_shape = [jax.ShapeDtypeStruct((m, n), F32)]
    args = [*xs, *([w] * n_x), res]
    scratch = []
    if with_stats:
        in_specs.append(pl.BlockSpec((1, tn), lambda i, j: (0, j)))
        out_specs += [pl.BlockSpec((tm, tn), lambda i, j: (i, j)),
                      pl.BlockSpec((tm, LANES), lambda i, j: (i, 0))]
        out_shape += [jax.ShapeDtypeStruct((m, n), BF16), jax.ShapeDtypeStruct((m, LANES), F32)]
        args.append(g_next.reshape(1, n))
        scratch = [pltpu.VMEM((tm, LANES), F32)]
    n_main = len(out_shape)
    out = pl.pallas_call(
        functools.partial(_mm_res_kernel, n_x=n_x, n_cast=len(cast_jobs), scale=scale,
                          with_stats=with_stats, d_model=n),
        grid=grid,
        in_specs=[*in_specs, *c_in],
        out_specs=[*out_specs, *c_out],
        out_shape=[*out_shape, *c_shape],
        scratch_shapes=scratch,
        compiler_params=_params(blk, 3 * _nbytes((tm, tn), F32)),
        name="matmul_residual",
    )(*args, *c_args)
    main = out[:n_main]
    return (main if with_stats else main[0]), out[n_main:]


def _softplus(z):
    return jnp.maximum(z, 0.0) + jnp.log1p(jnp.exp(-jnp.abs(z)))


def _lru_kernel(u_ref, g_ref, cw_ref, cb_ref, wa_ref, ba_ref, wx_ref, bx_ref, lam_ref,
                o_ref, uc_ref, h_ref, *, chunk):
    t_len = u_ref.shape[0]
    n_chunks = t_len // chunk
    n_blk = chunk // 8
    cw = cw_ref[...]
    cb = cb_ref[...]
    row8 = lax.broadcasted_iota(jnp.int32, (8, HEAD_DIM), 0)

    def coeffs(uc, d):
        ucb = uc.astype(BF16)
        gate_r = _dot(ucb, wa_ref[d]) + ba_ref[d:d + 1, :]
        gate_i = _dot(ucb, wx_ref[d]) + bx_ref[d:d + 1, :]
        r = jax.nn.sigmoid(gate_r)
        i = 0.5 + 0.5 * jnp.tanh(0.5 * gate_i)
        log_a = (-LRU_C * r) * _softplus(-lam_ref[d:d + 1, :])
        a = jnp.exp(log_a)
        th = jnp.tanh(log_a)
        n = -2.0 * th
        coef = jnp.where(n > 0.0, n * lax.rsqrt(n * (1.0 - th)), 0.0)
        b = coef * (i * uc)
        return a, b

    def tile_scan(a, b, reverse):
        for s in (1, 2, 4):
            if reverse:
                keep = row8 < 8 - s
                shift = 8 - s
            else:
                keep = row8 >= s
                shift = s
            a_sh = jnp.where(keep, pltpu.roll(a, shift, 0), 1.0)
            b_sh = jnp.where(keep, pltpu.roll(b, shift, 0), 0.0)
            b = a * b_sh + b
            a = a * a_sh
        return a, b

    def fwd_chunk(c, carry):
        c0 = pl.multiple_of(c * chunk, 8)
        main = u_ref[pl.ds(c0, chunk), :]
        prev = u_ref[pl.ds(jnp.maximum(c0 - 8, 0), 8), :]
        prev = jnp.where(c > 0, prev, 0.0)
        nxt = u_ref[pl.ds(jnp.minimum(c0 + chunk, t_len - 8), 8), :]
        nxt = jnp.where(c < n_chunks - 1, nxt, 0.0)
        x = jnp.concatenate([prev, main, nxt], axis=0)
        uc = (x[6:6 + chunk] * cw[0:1] + x[7:7 + chunk] * cw[1:2]
              + x[8:8 + chunk] * cw[2:3] + x[9:9 + chunk] * cw[3:4]) + cb
        uc_ref[pl.ds(c0, chunk), :] = uc
        a, b = coeffs(uc, 0)
        for k in range(n_blk):
            at, bt = tile_scan(a[8 * k:8 * k + 8], b[8 * k:8 * k + 8], False)
            ht = at * carry + bt
            h_ref[pl.ds(c0 + 8 * k, 8), :] = ht
            carry = jnp.broadcast_to(ht[7:8, :], (8, HEAD_DIM))
        return carry

    lax.fori_loop(0, n_chunks, fwd_chunk, jnp.zeros((8, HEAD_DIM), F32))

    def rev_chunk(ci, carry):
        c = n_chunks - 1 - ci
        c0 = pl.multiple_of(c * chunk, 8)
        uc = uc_ref[pl.ds(c0, chunk), :]
        a, b = coeffs(uc, 1)
        for k in reversed(range(n_blk)):
            at, bt = tile_scan(a[8 * k:8 * k + 8], b[8 * k:8 * k + 8], True)
            ht = at * carry + bt
            rows = pl.ds(c0 + 8 * k, 8)
            h_ref[rows, :] = h_ref[rows, :] + ht
            carry = jnp.broadcast_to(ht[0:1, :], (8, HEAD_DIM))
        rows = pl.ds(c0, chunk)
        o_ref[rows, :] = (h_ref[rows, :] * jax.nn.gelu(g_ref[rows, :])).astype(o_ref.dtype)
        return carry

    lax.fori_loop(0, n_chunks, rev_chunk, jnp.zeros((8, HEAD_DIM), F32))


def rglru(ug, conv_w, conv_b, w_a, b_a, w_x, b_x, lam, chunk=LRU_L):
    t_len = ug.shape[0]
    width = ug.shape[1] // 2
    nb = width // HEAD_DIM
    c = HEAD_DIM
    blk = (2 * _nbytes((t_len, c), F32) + _nbytes((t_len, c), BF16)
           + 4 * _nbytes((c, c), BF16))
    scratch = 2 * _nbytes((t_len, c), F32)
    return pl.pallas_call(
        functools.partial(_lru_kernel, chunk=chunk),
        grid=(nb,),
        in_specs=[pl.BlockSpec((t_len, c), lambda n: (0, n)),
                  pl.BlockSpec((t_len, c), lambda n: (0, n + nb)),
                  pl.BlockSpec((4, c), lambda n: (0, n)),
                  pl.BlockSpec((1, c), lambda n: (0, n)),
                  pl.BlockSpec((2, None, c, c), lambda n: (0, n, 0, 0)),
                  pl.BlockSpec((2, c), lambda n: (0, n)),
                  pl.BlockSpec((2, None, c, c), lambda n: (0, n, 0, 0)),
                  pl.BlockSpec((2, c), lambda n: (0, n)),
                  pl.BlockSpec((2, c), lambda n: (0, n))],
        out_specs=pl.BlockSpec((t_len, c), lambda n: (0, n)),
        out_shape=jax.ShapeDtypeStruct((t_len, width), BF16),
        scratch_shapes=[pltpu.VMEM((t_len, c), F32), pltpu.VMEM((t_len, c), F32)],
        compiler_params=_params(blk, scratch + 16 * _nbytes((chunk, c), F32)),
        name="rglru",
    )(ug, ug, conv_w, conv_b.reshape(1, width), w_a.astype(BF16), b_a, w_x.astype(BF16), b_x, lam)


_NT = (((1,), (1,)), ((), ()))


def _sink_softmax_pv(s, sink, v):
    d = HEAD_DIM
    fold = s[:, 0:LANES]
    for c in range(1, s.shape[1] // LANES):
        fold = jnp.maximum(fold, s[:, c * LANES:(c + 1) * LANES])
    m = jnp.maximum(jnp.max(fold, axis=1, keepdims=True), sink)
    p = jnp.exp(s - m).astype(BF16)
    pv = _dot(p, jnp.concatenate([v, jnp.ones_like(v)], axis=1))
    return pv[:, :d] / (pv[:, d:] + jnp.exp(sink - m))


def _swa_kernel(sink_ref, slope_ref, q_ref, k_ref, v_ref, o_ref):
    h = pl.program_id(0)
    n_real = q_ref.shape[0] - N_META
    n_blocks = n_real // BLOCK
    d = HEAD_DIM
    zpad = jnp.zeros((BLOCK - N_META, d), BF16)
    k_meta = jnp.concatenate([k_ref[0:N_META, :], zpad], axis=0)
    v_meta = jnp.concatenate([v_ref[0:N_META, :], zpad], axis=0)
    kw = 3 * BLOCK

    def per_head_rows(ref, rows):
        return jnp.concatenate([jnp.full((rows, 1), ref[GQA_GROUP * h + g], F32)
                                for g in range(GQA_GROUP)], axis=0)

    def stack_heads(x):
        return jnp.concatenate([x[:, g * d:(g + 1) * d] for g in range(GQA_GROUP)], axis=0)

    sink = per_head_rows(sink_ref, BLOCK)
    neg_slope = -per_head_rows(slope_ref, BLOCK)
    col = lax.broadcasted_iota(jnp.int32, (GQA_GROUP * BLOCK, BLOCK + kw), 1)
    rowq = lax.broadcasted_iota(jnp.int32, (GQA_GROUP * BLOCK, BLOCK + kw), 0) % BLOCK
    is_meta = col < N_META
    in_cols = col >= BLOCK
    col_off = col - BLOCK - rowq

    def bias_for(first_key_minus_first_query):
        rel = jnp.abs(col_off + first_key_minus_first_query)
        in_win = in_cols & (rel <= WINDOW)
        return jnp.where(is_meta, 0.0, jnp.where(in_win, neg_slope * rel.astype(F32), NEG_INF))

    def attend(j, ws, bias):
        r0, w0 = N_META + BLOCK * j, N_META + ws
        if not isinstance(j, int):
            r0, w0 = pl.multiple_of(r0, N_META), pl.multiple_of(w0, N_META)
        kcat = jnp.concatenate([k_meta, k_ref[pl.ds(w0, kw), :]], axis=0)
        vcat = jnp.concatenate([v_meta, v_ref[pl.ds(w0, kw), :]], axis=0)
        qs = stack_heads(q_ref[pl.ds(r0, BLOCK), :])
        s = lax.dot_general(qs, kcat, _NT, preferred_element_type=F32) * ATTN_SCALE + bias
        o = _sink_softmax_pv(s, sink, vcat)
        for g in range(GQA_GROUP):
            o_ref[pl.ds(r0, BLOCK), g * d:(g + 1) * d] = o[g * BLOCK:(g + 1) * BLOCK].astype(o_ref.dtype)

    attend(0, 0, bias_for(0))
    bias_mid = bias_for(-BLOCK)

    def block(j, carry):
        attend(j, BLOCK * (j - 1), bias_mid)
        return carry

    lax.fori_loop(1, n_blocks - 1, block, 0, unroll=2)
    attend(n_blocks - 1, n_real - kw, bias_for(n_real - kw - BLOCK * (n_blocks - 1)))

    kcat = jnp.concatenate([k_meta, k_ref[N_META:N_META + BLOCK, :]], axis=0)
    vcat = jnp.concatenate([v_meta, v_ref[N_META:N_META + BLOCK, :]], axis=0)
    colm = lax.broadcasted_iota(jnp.int32, (GQA_GROUP * N_META, 2 * BLOCK), 1)
    rowm = lax.broadcasted_iota(jnp.int32, (GQA_GROUP * N_META, 2 * BLOCK), 0) % N_META
    ok = (colm < N_META) | ((colm >= BLOCK) & (colm - BLOCK + N_META - rowm <= WINDOW))
    bias = jnp.where(ok, 0.0, NEG_INF)
    qs = stack_heads(q_ref[0:N_META, :])
    s = lax.dot_general(qs, kcat, _NT, preferred_element_type=F32) * ATTN_SCALE + bias
    o = _sink_softmax_pv(s, per_head_rows(sink_ref, N_META), vcat)
    for g in range(GQA_GROUP):
        o_ref[0:N_META, g * d:(g + 1) * d] = o[g * N_META:(g + 1) * N_META].astype(o_ref.dtype)


def windowed_attention(qkv, sink):
    t_len = qkv.shape[0]
    d = HEAD_DIM
    gw = GQA_GROUP * d
    slopes = jnp.asarray(2.0 ** (-8.0 * np.arange(1, SWA_Q_HEADS + 1) / SWA_Q_HEADS), F32)
    k_blk0 = SWA_Q_DIM // d
    v_blk0 = (SWA_Q_DIM + SWA_KV_DIM) // d
    blk = 2 * _nbytes((t_len, gw), BF16) + 2 * _nbytes((t_len, d), BF16)
    smem = pl.BlockSpec(memory_space=pltpu.SMEM)
    return pl.pallas_call(
        _swa_kernel,
        grid=(SWA_KV_HEADS,),
        in_specs=[smem, smem,
                  pl.BlockSpec((t_len, gw), lambda h: (0, h)),
                  pl.BlockSpec((t_len, d), lambda h: (0, k_blk0 + h)),
                  pl.BlockSpec((t_len, d), lambda h: (0, v_blk0 + h))],
        out_specs=pl.BlockSpec((t_len, gw), lambda h: (0, h)),
        out_shape=jax.ShapeDtypeStruct((t_len, SWA_Q_DIM), BF16),
        compiler_params=_params(blk, 8 * 2**20),
        name="windowed_attention",
    )(sink.astype(F32), slopes, qkv, qkv, qkv)


def _rope_tables(t_len, rows_pad):
    half = HEAD_DIM // 2
    n = t_len - N_META
    rows = n // GRID_W
    row = jnp.concatenate([jnp.zeros((N_META,), jnp.int32),
                           jnp.repeat(jnp.arange(rows, dtype=jnp.int32), GRID_W)])
    col = jnp.concatenate([jnp.zeros((N_META,), jnp.int32),
                           jnp.tile(jnp.arange(GRID_W, dtype=jnp.int32), rows)])
    inv_freq = ROPE_BASE ** (-(jnp.arange(half // 2, dtype=F32) * 2.0 / half))
    ang_row = row.astype(F32)[:, None] * inv_freq
    ang_col = col.astype(F32)[:, None] * inv_freq
    cos = jnp.concatenate([jnp.cos(ang_row), jnp.cos(ang_row), jnp.cos(ang_col), jnp.cos(ang_col)], axis=1)
    sin = jnp.concatenate([-jnp.sin(ang_row), jnp.sin(ang_row), -jnp.sin(ang_col), jnp.sin(ang_col)], axis=1)
    pad = ((0, rows_pad - t_len), (0, 0))
    return jnp.pad(cos, pad), jnp.pad(sin, pad)


def _qkv_prep_kernel(x_ref, cos_ref, sin_ref, qn_ref, kn_ref, q_ref, k_ref, v_ref, *, t_len):
    tm = x_ref.shape[0]
    d = HEAD_DIM
    cos = cos_ref[...]
    sin = sin_ref[...]
    lane = lax.broadcasted_iota(jnp.int32, (tm, d), 1)
    first = (lane % (d // 2)) < (d // 4)
    rows = pl.program_id(0) * tm + lax.broadcasted_iota(jnp.int32, (tm, 1), 0)
    valid = rows < t_len

    def norm_rope(xh, g):
        y = xh * lax.rsqrt(jnp.mean(xh * xh, axis=-1, keepdims=True) + RMS_EPS) * g
        partner = jnp.where(first, pltpu.roll(y, d - d // 4, 1), pltpu.roll(y, d // 4, 1))
        return y * cos + partner * sin

    qn = qn_ref[...]
    kn = kn_ref[...]
    k_norm_bound = (d ** 0.5) * jnp.max(jnp.abs(kn), axis=-1, keepdims=True)
    for hd in range(GA_Q_HEADS):
        qq = norm_rope(x_ref[:, hd * d:(hd + 1) * d], qn)
        shift = jnp.sqrt(jnp.sum(qq * qq, axis=-1, keepdims=True)) * k_norm_bound
        aux = jnp.where(lane == 0, -shift, jnp.where(lane == 1, -PAD_KEY_PENALTY, 0.0))
        q_ref[:, hd * EXT:hd * EXT + d] = qq.astype(q_ref.dtype)
        q_ref[:, hd * EXT + d:(hd + 1) * EXT] = aux.astype(q_ref.dtype)
    k_aux = jnp.where(lane == 0, 1.0, jnp.where((lane == 1) & jnp.logical_not(valid), 1.0, 0.0))
    ones = jnp.ones((tm, d), v_ref.dtype)
    for hd in range(GA_KV_HEADS):
        kk = norm_rope(x_ref[:, GA_Q_DIM + hd * d:GA_Q_DIM + (hd + 1) * d], kn)
        k_ref[:, hd * EXT:hd * EXT + d] = jnp.where(valid, kk, 0.0).astype(k_ref.dtype)
        k_ref[:, hd * EXT + d:(hd + 1) * EXT] = k_aux.astype(k_ref.dtype)
        v = x_ref[:, GA_Q_DIM + GA_KV_DIM + hd * d:GA_Q_DIM + GA_KV_DIM + (hd + 1) * d]
        v_ref[:, hd * EXT:hd * EXT + d] = jnp.where(valid, v, 0.0).astype(v_ref.dtype)
        v_ref[:, hd * EXT + d:(hd + 1) * EXT] = ones


def qkv_prep(qkv, q_norm, k_norm, cos, sin, keys_pad=KEYS_PAD, tm=PREP_TM):
    t_len, width = qkv.shape
    d = HEAD_DIM
    blk = (_nbytes((tm, width), F32) + 2 * _nbytes((tm, d), F32)
           + _nbytes((tm, GA_Q_HEADS * EXT), BF16) + 2 * _nbytes((tm, GA_KV_HEADS * EXT), BF16))
    return pl.pallas_call(
        functools.partial(_qkv_prep_kernel, t_len=t_len),
        grid=(keys_pad // tm,),
        in_specs=[pl.BlockSpec((tm, width), lambda i: (i, 0)),
                  pl.BlockSpec((tm, d), lambda i: (i, 0)),
                  pl.BlockSpec((tm, d), lambda i: (i, 0)),
                  pl.BlockSpec((1, d), lambda i: (0, 0)),
                  pl.BlockSpec((1, d), lambda i: (0, 0))],
        out_specs=[pl.BlockSpec((tm, GA_Q_HEADS * EXT), lambda i: (i, 0)),
                   pl.BlockSpec((tm, GA_KV_HEADS * EXT), lambda i: (i, 0)),
                   pl.BlockSpec((tm, GA_KV_HEADS * EXT), lambda i: (i, 0))],
        out_shape=[jax.ShapeDtypeStruct((t_len, GA_Q_HEADS * EXT), BF16),
                   jax.ShapeDtypeStruct((keys_pad, GA_KV_HEADS * EXT), BF16),
                   jax.ShapeDtypeStruct((keys_pad, GA_KV_HEADS * EXT), BF16)],
        compiler_params=_params(blk, 4 * 2**20),
        name="qkv_prep",
    )(qkv, cos, sin, q_norm.reshape(1, d), k_norm.reshape(1, d))


def _dense_fast_kernel(q_ref, k_ref, v_ref, o_ref, *, tk):
    d = HEAD_DIM
    q = q_ref[...]
    acc = None
    for c in range(k_ref.shape[0] // tk):
        s = lax.dot_general(q, k_ref[c * tk:(c + 1) * tk, :], _NT, preferred_element_type=F32)
        p = jnp.exp2(s * EXP2_SCALE).astype(BF16)
        pv = _dot(p, v_ref[c * tk:(c + 1) * tk, :])
        acc = pv if acc is None else acc + pv
    o_ref[...] = (acc[:, :d] / acc[:, d:]).astype(o_ref.dtype)


def _dense_safe_kernel(q_ref, k_ref, v_ref, o_ref, *, tk, n_keys):
    d = HEAD_DIM
    q = q_ref[...]
    rows = q.shape[0]

    def chunk(c, carry):
        m, acc = carry
        k0 = pl.multiple_of(c * tk, tk)
        t = lax.dot_general(q, k_ref[pl.ds(k0, tk), :], _NT, preferred_element_type=F32) * EXP2_SCALE
        key = k0 + lax.broadcasted_iota(jnp.int32, (rows, tk), 1)
        t = jnp.where(key < n_keys, t, NEG_INF)
        m_new = jnp.maximum(m, jnp.max(t, axis=1, keepdims=True))
        p = jnp.exp2(t - m_new).astype(BF16)
        acc = jnp.exp2(m - m_new) * acc + _dot(p, v_ref[pl.ds(k0, tk), :])
        return m_new, acc

    init = (jnp.full((rows, 1), NEG_INF, F32), jnp.zeros((rows, EXT), F32))
    _, acc = lax.fori_loop(0, k_ref.shape[0] // tk, chunk, init)
    o_ref[...] = (acc[:, :d] / acc[:, d:]).astype(o_ref.dtype)


def dense_attention(q_ext, k_ext, v_ext, fast, tq=DENSE_TQ, tk=DENSE_TK):
    t_len = q_ext.shape[0]
    d = HEAD_DIM
    kp = k_ext.shape[0]
    blk = _nbytes((tq, EXT), BF16) + _nbytes((tq, d), BF16) + 2 * _nbytes((kp, EXT), BF16)
    if fast:
        body = functools.partial(_dense_fast_kernel, tk=tk)
    else:
        body = functools.partial(_dense_safe_kernel, tk=tk, n_keys=t_len)
    return pl.pallas_call(
        body,
        grid=(GA_Q_HEADS, t_len // tq),
        in_specs=[pl.BlockSpec((tq, EXT), lambda h, i: (i, h)),
                  pl.BlockSpec((kp, EXT), lambda h, i: (0, h // GQA_GROUP)),
                  pl.BlockSpec((kp, EXT), lambda h, i: (0, h // GQA_GROUP))],
        out_specs=pl.BlockSpec((tq, d), lambda h, i: (i, h)),
        out_shape=jax.ShapeDtypeStruct((t_len, GA_Q_DIM), BF16),
        compiler_params=_params(blk, 8 * _nbytes((tq, tk), F32)),
        name="dense_attention_fast" if fast else "dense_attention_safe",
    )(q_ext, k_ext, v_ext)


def _mixer_ab(xg, rstd, i, w_in, conv_w, conv_b, w_a, b_a, w_x, b_x, lam, sink):
    ug, _ = matmul(xg, rstd, w_in, 0, 2 * LRU_WIDTH, F32, tn=1024)
    qkv, _ = matmul(xg, rstd, w_in, 2 * LRU_WIDTH, SWA_Q_DIM + 2 * SWA_KV_DIM, BF16, tn=512)
    y_a = rglru(ug, conv_w[i], conv_b[i], w_a[i], b_a[i], w_x[i], b_x[i], lam[i])
    y_b = windowed_attention(qkv, sink[i])
    return [y_a, y_b]


def _mixer_c(xg, rstd, i, w_in, q_norm, k_norm, cos, sin):
    qkv, _ = matmul(xg, rstd, w_in, 0, GA_Q_DIM + 2 * GA_KV_DIM, F32, tn=1024)
    ext = qkv_prep(qkv, q_norm[i], k_norm[i], cos, sin)
    logit_bound = ATTN_SCALE * HEAD_DIM * jnp.max(jnp.abs(q_norm[i])) * jnp.max(jnp.abs(k_norm[i]))
    o = lax.cond(logit_bound <= FAST_SOFTMAX_MAX_LOGIT,
                 lambda a: dense_attention(*a, fast=True),
                 lambda a: dense_attention(*a, fast=False), ext)
    return [o]


def kernel(x, meta_tokens, ffn1_norm, ffn1_w_gu, ffn1_w_down, mix_norm, ab_w_in, ab_w_out, lru_conv_w, lru_conv_b, lru_w_a, lru_b_a, lru_w_x, lru_b_x, lru_lambda, swa_sink, c_w_in, c_w_out, c_q_norm, c_k_norm, ffn2_norm, ffn2_w_gu, ffn2_w_down, final_norm):
    bsz, n, d = x.shape
    assert (bsz, n, d) == (1, SEQ, D_MODEL)
    cos, sin = _rope_tables(T_TOK, KEYS_PAD)
    h = jnp.concatenate([meta_tokens.astype(x.dtype), x[0]], axis=0)
    xg, rstd = norm_stats(h, ffn1_norm[0])
    w_gu1 = ffn1_w_gu[0].astype(BF16)
    for layer in range(DEPTH):
        i = layer // 2
        if layer % 2 == 0:
            in_job, out_job = (ab_w_in, i, 128, AB_IN // 2), (ab_w_out, i, 64, D_MODEL)
        else:
            in_job, out_job = (c_w_in, i, 128, C_IN // 3), (c_w_out, i, 64, D_MODEL)
        act, (w_dn1, w_in) = matmul_swiglu(xg, rstd, w_gu1,
                                           cast_jobs=[(ffn1_w_down, layer, 64, D_MODEL), in_job])
        (h, xg, rstd), (w_out,) = matmul_residual([act], w_dn1, h, 0.5, g_next=mix_norm[layer],
                                                  cast_jobs=[out_job])
        if layer % 2 == 0:
            ys = _mixer_ab(xg, rstd, i, w_in, lru_conv_w, lru_conv_b, lru_w_a, lru_b_a,
                           lru_w_x, lru_b_x, lru_lambda, swa_sink)
        else:
            ys = _mixer_c(xg, rstd, i, w_in, c_q_norm, c_k_norm, cos, sin)
        (h, xg, rstd), (w_gu2,) = matmul_residual(ys, w_out, h, 1.0, g_next=ffn2_norm[layer],
                                                  cast_jobs=[(ffn2_w_gu, layer, 64, 2 * D_FF)])
        last = layer + 1 == DEPTH
        jobs = [(ffn2_w_down, layer, 64, D_MODEL)]
        if not last:
            jobs.append((ffn1_w_gu, layer + 1, 128, D_MODEL))
        act, cast = matmul_swiglu(xg, rstd, w_gu2, cast_jobs=jobs)
        if last:
            h, _ = matmul_residual([act], cast[0], h, 0.5)
        else:
            w_gu1 = cast[1]
            (h, xg, rstd), _ = matmul_residual([act], cast[0], h, 0.5, g_next=ffn1_norm[layer + 1])
    return final_rmsnorm(h, final_norm)[None]
```

```python
import functools

import numpy as np
import jax
import jax.numpy as jnp
from jax import lax
from jax.experimental import pallas as pl
from jax.experimental.pallas import tpu as pltpu

F32 = jnp.float32
BF16 = jnp.bfloat16

D_MODEL = 4096
SEQ = 8192
DEPTH = 4
HEAD_DIM = 128
N_META = 16
T_TOK = SEQ + N_META
GRID_W = 64
BLOCK = 128
WINDOW = 128
RMS_EPS = 1e-6
NEG_INF = -1e30
D_FF = (3 * D_MODEL) // 2
LRU_WIDTH = D_MODEL // 2
LRU_C = 8.0
SWA_Q_HEADS = 16
SWA_KV_HEADS = 4
SWA_Q_DIM = SWA_Q_HEADS * HEAD_DIM
SWA_KV_DIM = SWA_KV_HEADS * HEAD_DIM
GA_Q_HEADS = 32
GA_KV_HEADS = 8
GA_Q_DIM = GA_Q_HEADS * HEAD_DIM
GA_KV_DIM = GA_KV_HEADS * HEAD_DIM
GQA_GROUP = 4
AB_IN = 2 * LRU_WIDTH + SWA_Q_DIM + 2 * SWA_KV_DIM
C_IN = GA_Q_DIM + 2 * GA_KV_DIM
ROPE_BASE = 10000.0
ATTN_SCALE = HEAD_DIM ** -0.5

V7X_VMEM_BYTES = 64 * 2**20
VMEM_CAP_BYTES = V7X_VMEM_BYTES - 6 * 2**20

TM = 912
LANES = 128
MXU_DIM = 256
KEYS_PAD = 33 * MXU_DIM
DENSE_TK = MXU_DIM
DENSE_TQ = 912
DENSE_HEADS = 2
PREP_TM = 384
NORM_TM = 432
LRU_L = 432
EXT = 2 * HEAD_DIM
EXP2_SCALE = ATTN_SCALE * float(np.log2(np.e))
PAD_KEY_PENALTY = 30000.0
FAST_SOFTMAX_MAX_LOGIT = 40.0


def _nbytes(shape, dtype):
    return int(np.prod(shape)) * jnp.dtype(dtype).itemsize


def _params(block_bytes, extra_bytes=0):
    need = 2 * block_bytes + extra_bytes + 4 * 2**20
    return pltpu.CompilerParams(vmem_limit_bytes=int(min(max(need, 16 * 2**20), VMEM_CAP_BYTES)))


def _norm_stats_kernel(x_ref, g_ref, xg_ref, rstd_ref):
    x = x_ref[...]
    xg_ref[...] = (x * g_ref[...]).astype(xg_ref.dtype)
    rstd = lax.rsqrt(jnp.mean(x * x, axis=-1, keepdims=True) + RMS_EPS)
    rstd_ref[...] = jnp.broadcast_to(rstd, rstd_ref.shape)


def norm_stats(x, g, tm=NORM_TM):
    m, d = x.shape
    blk = _nbytes((tm, d), F32) + _nbytes((tm, d), BF16) + _nbytes((tm, LANES), F32)
    return pl.pallas_call(
        _norm_stats_kernel,
        grid=(m // tm,),
        in_specs=[pl.BlockSpec((tm, d), lambda i: (i, 0)),
                  pl.BlockSpec((1, d), lambda i: (0, 0))],
        out_specs=[pl.BlockSpec((tm, d), lambda i: (i, 0)),
                   pl.BlockSpec((tm, LANES), lambda i: (i, 0))],
        out_shape=[jax.ShapeDtypeStruct((m, d), BF16),
                   jax.ShapeDtypeStruct((m, LANES), F32)],
        compiler_params=_params(blk, _nbytes((tm, d), F32)),
        name="norm_stats",
    )(x, g.reshape(1, d))


def _final_norm_kernel(a_ref, b_ref, g_ref, o_ref):
    x = jnp.concatenate([a_ref[N_META:, :], b_ref[...]], axis=0)
    y = x * lax.rsqrt(jnp.mean(x * x, axis=-1, keepdims=True) + RMS_EPS)
    o_ref[...] = y * g_ref[...]


def final_rmsnorm(h, g, tm=256):
    m, d = h.shape
    n = m - N_META
    blk = 2 * _nbytes((tm, d), F32) + _nbytes((N_META, d), F32)
    return pl.pallas_call(
        _final_norm_kernel,
        grid=(n // tm,),
        in_specs=[pl.BlockSpec((tm, d), lambda i: (i, 0)),
                  pl.BlockSpec((N_META, d), lambda i: ((i + 1) * (tm // N_META), 0)),
                  pl.BlockSpec((1, d), lambda i: (0, 0))],
        out_specs=pl.BlockSpec((tm, d), lambda i: (i, 0)),
        out_shape=jax.ShapeDtypeStruct((n, d), F32),
        compiler_params=_params(blk, 2 * _nbytes((tm, d), F32)),
        name="final_norm",
    )(h, h, g.reshape(1, d))


def _dot(a, b):
    return jnp.dot(a, b, preferred_element_type=F32)


def _lane_tile(v, width):
    return jnp.concatenate([v] * (width // LANES), axis=1)


def _cast_specs(jobs, n_col_tiles, steps):
    in_specs, out_specs, out_shape, args, nbytes = [], [], [], [], 0
    for w, layer, rb, cb in jobs:
        _, k, n = w.shape
        assert k % rb == 0 and n % cb == 0 and (k // rb) * (n // cb) <= steps
        nc = n // cb
        last = (k // rb) * nc - 1

        def slab(i, j, nc=nc, last=last):
            t = jnp.minimum(i * n_col_tiles + j, last)
            return t // nc, t % nc

        in_specs.append(pl.BlockSpec((None, rb, cb),
                                     functools.partial(lambda i, j, f, l: (l, *f(i, j)), f=slab, l=layer)))
        out_specs.append(pl.BlockSpec((rb, cb), slab))
        out_shape.append(jax.ShapeDtypeStruct((k, n), BF16))
        args.append(w)
        nbytes += _nbytes((rb, cb), F32) + _nbytes((rb, cb), BF16)
    return in_specs, out_specs, out_shape, args, nbytes


def _run_cast_jobs(src_refs, dst_refs):
    for src, dst in zip(src_refs, dst_refs):
        dst[...] = src[...].astype(dst.dtype)


def _mm_kernel(*refs, n_cast):
    x_ref, rstd_ref, w_ref = refs[:3]
    o_ref = refs[3 + n_cast]
    _run_cast_jobs(refs[3:3 + n_cast], refs[4 + n_cast:])
    acc = _dot(x_ref[...], w_ref[...])
    o_ref[...] = (acc * _lane_tile(rstd_ref[...], acc.shape[1])).astype(o_ref.dtype)


def matmul(xg, rstd, w, n_off, n, out_dtype, tn, cast_jobs=(), tm=TM):
    m, k = xg.shape
    joff = n_off // tn
    grid = (m // tm, n // tn)
    c_in, c_out, c_shape, c_args, c_bytes = _cast_specs(cast_jobs, grid[1], grid[0] * grid[1])
    blk = (_nbytes((tm, k), BF16) + _nbytes((tm, LANES), F32) + _nbytes((k, tn), BF16)
           + _nbytes((tm, tn), out_dtype) + c_bytes)
    out = pl.pallas_call(
        functools.partial(_mm_kernel, n_cast=len(cast_jobs)),
        grid=grid,
        in_specs=[pl.BlockSpec((tm, k), lambda i, j: (i, 0)),
                  pl.BlockSpec((tm, LANES), lambda i, j: (i, 0)),
                  pl.BlockSpec((k, tn), lambda i, j: (0, j + joff)), *c_in],
        out_specs=[pl.BlockSpec((tm, tn), lambda i, j: (i, j)), *c_out],
        out_shape=[jax.ShapeDtypeStruct((m, n), out_dtype), *c_shape],
        compiler_params=_params(blk, 2 * _nbytes((tm, tn), F32)),
        name="matmul",
    )(xg, rstd, w, *c_args)
    return out[0], out[1:]


def _mm_swiglu_kernel(*refs, n_cast):
    x_ref, rstd_ref, wg_ref, wu_ref = refs[:4]
    o_ref = refs[4 + n_cast]
    _run_cast_jobs(refs[4:4 + n_cast], refs[5 + n_cast:])
    x = x_ref[...]
    rstd = _lane_tile(rstd_ref[...], o_ref.shape[1])
    gate = _dot(x, wg_ref[...]) * rstd
    up = _dot(x, wu_ref[...]) * rstd
    o_ref[...] = (jax.nn.silu(gate) * up).astype(o_ref.dtype)


def matmul_swiglu(xg, rstd, w_gu, cast_jobs=(), tn=512, tm=TM):
    m, k = xg.shape
    f = w_gu.shape[1] // 2
    uoff = f // tn
    grid = (m // tm, f // tn)
    c_in, c_out, c_shape, c_args, c_bytes = _cast_specs(cast_jobs, grid[1], grid[0] * grid[1])
    blk = (_nbytes((tm, k), BF16) + _nbytes((tm, LANES), F32) + 2 * _nbytes((k, tn), BF16)
           + _nbytes((tm, tn), BF16) + c_bytes)
    out = pl.pallas_call(
        functools.partial(_mm_swiglu_kernel, n_cast=len(cast_jobs)),
        grid=grid,
        in_specs=[pl.BlockSpec((tm, k), lambda i, j: (i, 0)),
                  pl.BlockSpec((tm, LANES), lambda i, j: (i, 0)),
                  pl.BlockSpec((k, tn), lambda i, j: (0, j)),
                  pl.BlockSpec((k, tn), lambda i, j: (0, j + uoff)), *c_in],
        out_specs=[pl.BlockSpec((tm, tn), lambda i, j: (i, j)), *c_out],
        out_shape=[jax.ShapeDtypeStruct((m, f), BF16), *c_shape],
        compiler_params=_params(blk, 3 * _nbytes((tm, tn), F32)),
        name="matmul_swiglu",
    )(xg, rstd, w_gu, w_gu, *c_args)
    return out[0], out[1:]


def _mm_res_kernel(*refs, n_x, n_cast, scale, with_stats, d_model):
    x_refs, w_refs = refs[:n_x], refs[n_x:2 * n_x]
    r_ref = refs[2 * n_x]
    n_in = 2 * n_x + 1 + int(with_stats)
    n_out = 3 if with_stats else 1
    outs = refs[n_in + n_cast:n_in + n_cast + n_out]
    _run_cast_jobs(refs[n_in:n_in + n_cast], refs[n_in + n_cast + n_out:n_in + 2 * n_cast + n_out])
    acc = _dot(x_refs[0][...], w_refs[0][...])
    for x_ref, w_ref in zip(x_refs[1:], w_refs[1:]):
        acc = acc + _dot(x_ref[...], w_ref[...])
    h = r_ref[...] + scale * acc
    if not with_stats:
        outs[0][...] = h
        return
    g_ref, ssq_ref = refs[n_in - 1], refs[-1]
    o_ref, xg_ref, rstd_ref = outs
    o_ref[...] = h
    xg_ref[...] = (h * g_ref[...]).astype(xg_ref.dtype)
    sq = h * h
    part = sq[:, 0:LANES]
    for c in range(1, sq.shape[1] // LANES):
        part = part + sq[:, c * LANES:(c + 1) * LANES]
    j = pl.program_id(1)

    @pl.when(j == 0)
    def _():
        ssq_ref[...] = part

    @pl.when(j > 0)
    def _():
        ssq_ref[...] += part

    @pl.when(j == pl.num_programs(1) - 1)
    def _():
        tot = jnp.sum(ssq_ref[...], axis=-1, keepdims=True)
        rstd_ref[...] = jnp.broadcast_to(lax.rsqrt(tot * (1.0 / d_model) + RMS_EPS), rstd_ref.shape)


def matmul_residual(xs, w, res, scale, g_next=None, cast_jobs=(), tn=512, tm=TM):
    m, k = xs[0].shape
    n_x = len(xs)
    assert all(x.shape == (m, k) for x in xs)
    n = w.shape[1]
    with_stats = g_next is not None
    grid = (m // tm, n // tn)
    c_in, c_out, c_shape, c_args, c_bytes = _cast_specs(cast_jobs, grid[1], grid[0] * grid[1])
    blk = (n_x * (_nbytes((tm, k), BF16) + _nbytes((k, tn), BF16)) + 2 * _nbytes((tm, tn), F32)
           + _nbytes((tm, tn), BF16) + _nbytes((tm, LANES), F32) + c_bytes)
    in_specs = [pl.BlockSpec((tm, k), lambda i, j: (i, 0)) for _ in xs]
    in_specs += [pl.BlockSpec((k, tn), functools.partial(lambda i, j, c: (c, j), c=c))
                 for c in range(n_x)]
    in_specs.append(pl.BlockSpec((tm, tn), lambda i, j: (i, j)))
    out_specs = [pl.BlockSpec((tm, tn), lambda i, j: (i, j))]
    out_shape = [jax.ShapeDtypeStruct((m, n), F32)]
    args = [*xs, *([w] * n_x), res]
    scratch = []
    if with_stats:
        in_specs.append(pl.BlockSpec((1, tn), lambda i, j: (0, j)))
        out_specs += [pl.BlockSpec((tm, tn), lambda i, j: (i, j)),
                      pl.BlockSpec((tm, LANES), lambda i, j: (i, 0))]
        out_shape += [jax.ShapeDtypeStruct((m, n), BF16), jax.ShapeDtypeStruct((m, LANES), F32)]
        args.append(g_next.reshape(1, n))
        scratch = [pltpu.VMEM((tm, LANES), F32)]
    n_main = len(out_shape)
    out = pl.pallas_call(
        functools.partial(_mm_res_kernel, n_x=n_x, n_cast=len(cast_jobs), scale=scale,
                          with_stats=with_stats, d_model=n),
        grid=grid,
        in_specs=[*in_specs, *c_in],
        out_specs=[*out_specs, *c_out],
        out_shape=[*out_shape, *c_shape],
        scratch_shapes=scratch,
        compiler_params=_params(blk, 3 * _nbytes((tm, tn), F32)),
        name="matmul_residual",
    )(*args, *c_args)
    main = out[:n_main]
    return (main if with_stats else main[0]), out[n_main:]


def _softplus(z):
    return jnp.maximum(z, 0.0) + jnp.log1p(jnp.exp(-jnp.abs(z)))


def _lru_kernel(u_ref, g_ref, cw_ref, cb_ref, wa_ref, ba_ref, wx_ref, bx_ref, lam_ref,
                o_ref, uc_ref, h_ref, *, chunk):
    t_len = u_ref.shape[0]
    n_chunks = t_len // chunk
    n_blk = chunk // 8
    cw = cw_ref[...]
    cb = cb_ref[...]
    row8 = lax.broadcasted_iota(jnp.int32, (8, HEAD_DIM), 0)

    def coeffs(uc, d):
        ucb = uc.astype(BF16)
        gate_r = _dot(ucb, wa_ref[d]) + ba_ref[d:d + 1, :]
        gate_i = _dot(ucb, wx_ref[d]) + bx_ref[d:d + 1, :]
        r = jax.nn.sigmoid(gate_r)
        i = 0.5 + 0.5 * jnp.tanh(0.5 * gate_i)
        log_a = (-LRU_C * r) * _softplus(-lam_ref[d:d + 1, :])
        a = jnp.exp(log_a)
        th = jnp.tanh(log_a)
        n = -2.0 * th
        coef = jnp.where(n > 0.0, n * lax.rsqrt(n * (1.0 - th)), 0.0)
        b = coef * (i * uc)
        return a, b

    def tile_scan(a, b, reverse):
        for s in (1, 2, 4):
            if reverse:
                keep = row8 < 8 - s
                shift = 8 - s
            else:
                keep = row8 >= s
                shift = s
            a_sh = jnp.where(keep, pltpu.roll(a, shift, 0), 1.0)
            b_sh = jnp.where(keep, pltpu.roll(b, shift, 0), 0.0)
            b = a * b_sh + b
            a = a * a_sh
        return a, b

    def fwd_chunk(c, carry):
        c0 = pl.multiple_of(c * chunk, 8)
        main = u_ref[pl.ds(c0, chunk), :]
        prev = u_ref[pl.ds(jnp.maximum(c0 - 8, 0), 8), :]
        prev = jnp.where(c > 0, prev, 0.0)
        nxt = u_ref[pl.ds(jnp.minimum(c0 + chunk, t_len - 8), 8), :]
        nxt = jnp.where(c < n_chunks - 1, nxt, 0.0)
        x = jnp.concatenate([prev, main, nxt], axis=0)
        uc = (x[6:6 + chunk] * cw[0:1] + x[7:7 + chunk] * cw[1:2]
              + x[8:8 + chunk] * cw[2:3] + x[9:9 + chunk] * cw[3:4]) + cb
        uc_ref[pl.ds(c0, chunk), :] = uc
        a, b = coeffs(uc, 0)
        for k in range(n_blk):
            at, bt = tile_scan(a[8 * k:8 * k + 8], b[8 * k:8 * k + 8], False)
            ht = at * carry + bt
            h_ref[pl.ds(c0 + 8 * k, 8), :] = ht
            carry = jnp.broadcast_to(ht[7:8, :], (8, HEAD_DIM))
        return carry

    lax.fori_loop(0, n_chunks, fwd_chunk, jnp.zeros((8, HEAD_DIM), F32))

    def rev_chunk(ci, carry):
        c = n_chunks - 1 - ci
        c0 = pl.multiple_of(c * chunk, 8)
        uc = uc_ref[pl.ds(c0, chunk), :]
        a, b = coeffs(uc, 1)
        for k in reversed(range(n_blk)):
            at, bt = tile_scan(a[8 * k:8 * k + 8], b[8 * k:8 * k + 8], True)
            ht = at * carry + bt
            rows = pl.ds(c0 + 8 * k, 8)
            h_ref[rows, :] = h_ref[rows, :] + ht
            carry = jnp.broadcast_to(ht[0:1, :], (8, HEAD_DIM))
        rows = pl.ds(c0, chunk)
        o_ref[rows, :] = (h_ref[rows, :] * jax.nn.gelu(g_ref[rows, :])).astype(o_ref.dtype)
        return carry

    lax.fori_loop(0, n_chunks, rev_chunk, jnp.zeros((8, HEAD_DIM), F32))


def rglru(ug, conv_w, conv_b, w_a, b_a, w_x, b_x, lam, chunk=LRU_L):
    t_len = ug.shape[0]
    width = ug.shape[1] // 2
    nb = width // HEAD_DIM
    c = HEAD_DIM
    blk = (2 * _nbytes((t_len, c), F32) + _nbytes((t_len, c), BF16)
           + 4 * _nbytes((c, c), BF16))
    scratch = 2 * _nbytes((t_len, c), F32)
    return pl.pallas_call(
        functools.partial(_lru_kernel, chunk=chunk),
        grid=(nb,),
        in_specs=[pl.BlockSpec((t_len, c), lambda n: (0, n)),
                  pl.BlockSpec((t_len, c), lambda n: (0, n + nb)),
                  pl.BlockSpec((4, c), lambda n: (0, n)),
                  pl.BlockSpec((1, c), lambda n: (0, n)),
                  pl.BlockSpec((2, None, c, c), lambda n: (0, n, 0, 0)),
                  pl.BlockSpec((2, c), lambda n: (0, n)),
                  pl.BlockSpec((2, None, c, c), lambda n: (0, n, 0, 0)),
                  pl.BlockSpec((2, c), lambda n: (0, n)),
                  pl.BlockSpec((2, c), lambda n: (0, n))],
        out_specs=pl.BlockSpec((t_len, c), lambda n: (0, n)),
        out_shape=jax.ShapeDtypeStruct((t_len, width), BF16),
        scratch_shapes=[pltpu.VMEM((t_len, c), F32), pltpu.VMEM((t_len, c), F32)],
        compiler_params=_params(blk, scratch + 16 * _nbytes((chunk, c), F32)),
        name="rglru",
    )(ug, ug, conv_w, conv_b.reshape(1, width), w_a.astype(BF16), b_a, w_x.astype(BF16), b_x, lam)


_NT = (((1,), (1,)), ((), ()))


def _sink_softmax_pv(s, sink, v):
    d = HEAD_DIM
    fold = s[:, 0:LANES]
    for c in range(1, s.shape[1] // LANES):
        fold = jnp.maximum(fold, s[:, c * LANES:(c + 1) * LANES])
    m = jnp.maximum(jnp.max(fold, axis=1, keepdims=True), sink)
    p = jnp.exp(s - m).astype(BF16)
    pv = _dot(p, jnp.concatenate([v, jnp.ones_like(v)], axis=1))
    return pv[:, :d] / (pv[:, d:] + jnp.exp(sink - m))


def _swa_kernel(sink_ref, slope_ref, q_ref, k_ref, v_ref, o_ref):
    h = pl.program_id(0)
    n_real = q_ref.shape[0] - N_META
    n_blocks = n_real // BLOCK
    d = HEAD_DIM
    zpad = jnp.zeros((BLOCK - N_META, d), BF16)
    k_meta = jnp.concatenate([k_ref[0:N_META, :], zpad], axis=0)
    v_meta = jnp.concatenate([v_ref[0:N_META, :], zpad], axis=0)
    kw = 3 * BLOCK

    def per_head_rows(ref, rows):
        return jnp.concatenate([jnp.full((rows, 1), ref[GQA_GROUP * h + g], F32)
                                for g in range(GQA_GROUP)], axis=0)

    def stack_heads(x):
        return jnp.concatenate([x[:, g * d:(g + 1) * d] for g in range(GQA_GROUP)], axis=0)

    sink = per_head_rows(sink_ref, BLOCK)
    neg_slope = -per_head_rows(slope_ref, BLOCK)
    col = lax.broadcasted_iota(jnp.int32, (GQA_GROUP * BLOCK, BLOCK + kw), 1)
    rowq = lax.broadcasted_iota(jnp.int32, (GQA_GROUP * BLOCK, BLOCK + kw), 0) % BLOCK
    is_meta = col < N_META
    in_cols = col >= BLOCK
    col_off = col - BLOCK - rowq

    def bias_for(first_key_minus_first_query):
        rel = jnp.abs(col_off + first_key_minus_first_query)
        in_win = in_cols & (rel <= WINDOW)
        return jnp.where(is_meta, 0.0, jnp.where(in_win, neg_slope * rel.astype(F32), NEG_INF))

    def attend(j, ws, bias):
        r0, w0 = N_META + BLOCK * j, N_META + ws
        if not isinstance(j, int):
            r0, w0 = pl.multiple_of(r0, N_META), pl.multiple_of(w0, N_META)
        kcat = jnp.concatenate([k_meta, k_ref[pl.ds(w0, kw), :]], axis=0)
        vcat = jnp.concatenate([v_meta, v_ref[pl.ds(w0, kw), :]], axis=0)
        qs = stack_heads(q_ref[pl.ds(r0, BLOCK), :])
        s = lax.dot_general(qs, kcat, _NT, preferred_element_type=F32) * ATTN_SCALE + bias
        o = _sink_softmax_pv(s, sink, vcat)
        for g in range(GQA_GROUP):
            o_ref[pl.ds(r0, BLOCK), g * d:(g + 1) * d] = o[g * BLOCK:(g + 1) * BLOCK].astype(o_ref.dtype)

    attend(0, 0, bias_for(0))
    bias_mid = bias_for(-BLOCK)

    def block(j, carry):
        attend(j, BLOCK * (j - 1), bias_mid)
        return carry

    lax.fori_loop(1, n_blocks - 1, block, 0, unroll=2)
    attend(n_blocks - 1, n_real - kw, bias_for(n_real - kw - BLOCK * (n_blocks - 1)))

    kcat = jnp.concatenate([k_meta, k_ref[N_META:N_META + BLOCK, :]], axis=0)
    vcat = jnp.concatenate([v_meta, v_ref[N_META:N_META + BLOCK, :]], axis=0)
    colm = lax.broadcasted_iota(jnp.int32, (GQA_GROUP * N_META, 2 * BLOCK), 1)
    rowm = lax.broadcasted_iota(jnp.int32, (GQA_GROUP * N_META, 2 * BLOCK), 0) % N_META
    ok = (colm < N_META) | ((colm >= BLOCK) & (colm - BLOCK + N_META - rowm <= WINDOW))
    bias = jnp.where(ok, 0.0, NEG_INF)
    qs = stack_heads(q_ref[0:N_META, :])
    s = lax.dot_general(qs, kcat, _NT, preferred_element_type=F32) * ATTN_SCALE + bias
    o = _sink_softmax_pv(s, per_head_rows(sink_ref, N_META), vcat)
    for g in range(GQA_GROUP):
        o_ref[0:N_META, g * d:(g + 1) * d] = o[g * N_META:(g + 1) * N_META].astype(o_ref.dtype)


def windowed_attention(qkv, sink):
    t_len = qkv.shape[0]
    d = HEAD_DIM
    gw = GQA_GROUP * d
    slopes = jnp.asarray(2.0 ** (-8.0 * np.arange(1, SWA_Q_HEADS + 1) / SWA_Q_HEADS), F32)
    k_blk0 = SWA_Q_DIM // d
    v_blk0 = (SWA_Q_DIM + SWA_KV_DIM) // d
    blk = 2 * _nbytes((t_len, gw), BF16) + 2 * _nbytes((t_len, d), BF16)
    smem = pl.BlockSpec(memory_space=pltpu.SMEM)
    return pl.pallas_call(
        _swa_kernel,
        grid=(SWA_KV_HEADS,),
        in_specs=[smem, smem,
                  pl.BlockSpec((t_len, gw), lambda h: (0, h)),
                  pl.BlockSpec((t_len, d), lambda h: (0, k_blk0 + h)),
                  pl.BlockSpec((t_len, d), lambda h: (0, v_blk0 + h))],
        out_specs=pl.BlockSpec((t_len, gw), lambda h: (0, h)),
        out_shape=jax.ShapeDtypeStruct((t_len, SWA_Q_DIM), BF16),
        compiler_params=_params(blk, 8 * 2**20),
        name="windowed_attention",
    )(sink.astype(F32), slopes, qkv, qkv, qkv)


def _rope_tables(t_len, rows_pad):
    half = HEAD_DIM // 2
    n = t_len - N_META
    rows = n // GRID_W
    row = jnp.concatenate([jnp.zeros((N_META,), jnp.int32),
                           jnp.repeat(jnp.arange(rows, dtype=jnp.int32), GRID_W)])
    col = jnp.concatenate([jnp.zeros((N_META,), jnp.int32),
                           jnp.tile(jnp.arange(GRID_W, dtype=jnp.int32), rows)])
    inv_freq = ROPE_BASE ** (-(jnp.arange(half // 2, dtype=F32) * 2.0 / half))
    ang_row = row.astype(F32)[:, None] * inv_freq
    ang_col = col.astype(F32)[:, None] * inv_freq
    cos = jnp.concatenate([jnp.cos(ang_row), jnp.cos(ang_row), jnp.cos(ang_col), jnp.cos(ang_col)], axis=1)
    sin = jnp.concatenate([-jnp.sin(ang_row), jnp.sin(ang_row), -jnp.sin(ang_col), jnp.sin(ang_col)], axis=1)
    pad = ((0, rows_pad - t_len), (0, 0))
    return jnp.pad(cos, pad), jnp.pad(sin, pad)


def _qkv_prep_kernel(x_ref, cos_ref, sin_ref, qn_ref, kn_ref, q_ref, k_ref, v_ref, *, t_len):
    tm = x_ref.shape[0]
    d = HEAD_DIM
    cos = cos_ref[...]
    sin = sin_ref[...]
    lane = lax.broadcasted_iota(jnp.int32, (tm, d), 1)
    first = (lane % (d // 2)) < (d // 4)
    rows = pl.program_id(0) * tm + lax.broadcasted_iota(jnp.int32, (tm, 1), 0)
    valid = rows < t_len

    def norm_rope(xh, g):
        y = xh * lax.rsqrt(jnp.mean(xh * xh, axis=-1, keepdims=True) + RMS_EPS) * g
        partner = jnp.where(first, pltpu.roll(y, d - d // 4, 1), pltpu.roll(y, d // 4, 1))
        return y * cos + partner * sin

    qn = qn_ref[...]
    kn = kn_ref[...]
    for hd in range(GA_Q_HEADS):
        q_ref[:, hd * d:(hd + 1) * d] = norm_rope(x_ref[:, hd * d:(hd + 1) * d], qn).astype(q_ref.dtype)
    k_aux = jnp.where(lane == 0, 1.0, jnp.where((lane == 1) & jnp.logical_not(valid), 1.0, 0.0))
    ones = jnp.ones((tm, d), v_ref.dtype)
    for hd in range(GA_KV_HEADS):
        kk = norm_rope(x_ref[:, GA_Q_DIM + hd * d:GA_Q_DIM + (hd + 1) * d], kn)
        k_ref[:, hd * EXT:hd * EXT + d] = jnp.where(valid, kk, 0.0).astype(k_ref.dtype)
        k_ref[:, hd * EXT + d:(hd + 1) * EXT] = k_aux.astype(k_ref.dtype)
        v = x_ref[:, GA_Q_DIM + GA_KV_DIM + hd * d:GA_Q_DIM + GA_KV_DIM + (hd + 1) * d]
        v_ref[:, hd * EXT:hd * EXT + d] = jnp.where(valid, v, 0.0).astype(v_ref.dtype)
        v_ref[:, hd * EXT + d:(hd + 1) * EXT] = ones


def qkv_prep(qkv, q_norm, k_norm, cos, sin, keys_pad=KEYS_PAD, tm=PREP_TM):
    t_len, width = qkv.shape
    d = HEAD_DIM
    blk = (_nbytes((tm, width), F32) + 2 * _nbytes((tm, d), F32)
           + _nbytes((tm, GA_Q_DIM), BF16) + 2 * _nbytes((tm, GA_KV_HEADS * EXT), BF16))
    return pl.pallas_call(
        functools.partial(_qkv_prep_kernel, t_len=t_len),
        grid=(keys_pad // tm,),
        in_specs=[pl.BlockSpec((tm, width), lambda i: (i, 0)),
                  pl.BlockSpec((tm, d), lambda i: (i, 0)),
                  pl.BlockSpec((tm, d), lambda i: (i, 0)),
                  pl.BlockSpec((1, d), lambda i: (0, 0)),
                  pl.BlockSpec((1, d), lambda i: (0, 0))],
        out_specs=[pl.BlockSpec((tm, GA_Q_DIM), lambda i: (i, 0)),
                   pl.BlockSpec((tm, GA_KV_HEADS * EXT), lambda i: (i, 0)),
                   pl.BlockSpec((tm, GA_KV_HEADS * EXT), lambda i: (i, 0))],
        out_shape=[jax.ShapeDtypeStruct((t_len, GA_Q_DIM), BF16),
                   jax.ShapeDtypeStruct((keys_pad, GA_KV_HEADS * EXT), BF16),
                   jax.ShapeDtypeStruct((keys_pad, GA_KV_HEADS * EXT), BF16)],
        compiler_params=_params(blk, 4 * 2**20),
        name="qkv_prep",
    )(qkv, cos, sin, q_norm.reshape(1, d), k_norm.reshape(1, d))


def _stacked_q_ext(q_ref, shift_ref):
    d = HEAD_DIM
    q = jnp.concatenate([q_ref[:, g * d:(g + 1) * d] for g in range(q_ref.shape[1] // d)], axis=0)
    lane = lax.broadcasted_iota(jnp.int32, q.shape, 1)
    aux = jnp.where(lane == 0, -shift_ref[0], jnp.where(lane == 1, -PAD_KEY_PENALTY, 0.0))
    return jnp.concatenate([q, aux.astype(q.dtype)], axis=1)


def _store_heads(o_ref, acc):
    d = HEAD_DIM
    tq = o_ref.shape[0]
    o = acc[:, :d] / acc[:, d:]
    for g in range(o_ref.shape[1] // d):
        o_ref[:, g * d:(g + 1) * d] = o[g * tq:(g + 1) * tq].astype(o_ref.dtype)


def _dense_fast_kernel(shift_ref, q_ref, k_ref, v_ref, o_ref, *, tk):
    q = _stacked_q_ext(q_ref, shift_ref)
    acc = None
    for c in range(k_ref.shape[0] // tk):
        s = lax.dot_general(q, k_ref[c * tk:(c + 1) * tk, :], _NT, preferred_element_type=F32)
        p = jnp.exp2(s * EXP2_SCALE).astype(BF16)
        pv = _dot(p, v_ref[c * tk:(c + 1) * tk, :])
        acc = pv if acc is None else acc + pv
    _store_heads(o_ref, acc)


def _dense_safe_kernel(shift_ref, q_ref, k_ref, v_ref, o_ref, *, tk, n_keys):
    q = _stacked_q_ext(q_ref, shift_ref)
    rows = q.shape[0]

    def chunk(c, carry):
        m, acc = carry
        k0 = pl.multiple_of(c * tk, tk)
        t = lax.dot_general(q, k_ref[pl.ds(k0, tk), :], _NT, preferred_element_type=F32) * EXP2_SCALE
        key = k0 + lax.broadcasted_iota(jnp.int32, (rows, tk), 1)
        t = jnp.where(key < n_keys, t, NEG_INF)
        m_new = jnp.maximum(m, jnp.max(t, axis=1, keepdims=True))
        p = jnp.exp2(t - m_new).astype(BF16)
        acc = jnp.exp2(m - m_new) * acc + _dot(p, v_ref[pl.ds(k0, tk), :])
        return m_new, acc

    init = (jnp.full((rows, 1), NEG_INF, F32), jnp.zeros((rows, EXT), F32))
    _, acc = lax.fori_loop(0, k_ref.shape[0] // tk, chunk, init)
    _store_heads(o_ref, acc)


def dense_attention(q, k_ext, v_ext, shift, fast, tq=DENSE_TQ, tk=DENSE_TK, heads=DENSE_HEADS):
    t_len = q.shape[0]
    d = HEAD_DIM
    kp = k_ext.shape[0]
    assert GQA_GROUP % heads == 0
    groups_per_kv = GQA_GROUP // heads
    blk = 2 * _nbytes((tq, heads * d), BF16) + 2 * _nbytes((kp, EXT), BF16)
    if fast:
        body = functools.partial(_dense_fast_kernel, tk=tk)
    else:
        body = functools.partial(_dense_safe_kernel, tk=tk, n_keys=t_len)
    return pl.pallas_call(
        body,
        grid=(GA_Q_HEADS // heads, t_len // tq),
        in_specs=[pl.BlockSpec(memory_space=pltpu.SMEM),
                  pl.BlockSpec((tq, heads * d), lambda h, i: (i, h)),
                  pl.BlockSpec((kp, EXT), lambda h, i: (0, h // groups_per_kv)),
                  pl.BlockSpec((kp, EXT), lambda h, i: (0, h // groups_per_kv))],
        out_specs=pl.BlockSpec((tq, heads * d), lambda h, i: (i, h)),
        out_shape=jax.ShapeDtypeStruct((t_len, GA_Q_DIM), BF16),
        compiler_params=_params(blk, 10 * _nbytes((heads * tq, tk), F32)),
        name="dense_attention_fast" if fast else "dense_attention_safe",
    )(shift, q, k_ext, v_ext)


def _mixer_ab(xg, rstd, i, w_in, conv_w, conv_b, w_a, b_a, w_x, b_x, lam, sink):
    ug, _ = matmul(xg, rstd, w_in, 0, 2 * LRU_WIDTH, F32, tn=1024)
    qkv, _ = matmul(xg, rstd, w_in, 2 * LRU_WIDTH, SWA_Q_DIM + 2 * SWA_KV_DIM, BF16, tn=512)
    y_a = rglru(ug, conv_w[i], conv_b[i], w_a[i], b_a[i], w_x[i], b_x[i], lam[i])
    y_b = windowed_attention(qkv, sink[i])
    return [y_a, y_b]


def _mixer_c(xg, rstd, i, w_in, q_norm, k_norm, cos, sin):
    qkv, _ = matmul(xg, rstd, w_in, 0, GA_Q_DIM + 2 * GA_KV_DIM, F32, tn=1024)
    q, k_ext, v_ext = qkv_prep(qkv, q_norm[i], k_norm[i], cos, sin)
    shift = (HEAD_DIM * jnp.max(jnp.abs(q_norm[i])) * jnp.max(jnp.abs(k_norm[i]))).reshape(1)
    o = lax.cond(ATTN_SCALE * shift[0] <= FAST_SOFTMAX_MAX_LOGIT,
                 lambda a: dense_attention(*a, fast=True),
                 lambda a: dense_attention(*a, fast=False), (q, k_ext, v_ext, shift))
    return [o]


def kernel(x, meta_tokens, ffn1_norm, ffn1_w_gu, ffn1_w_down, mix_norm, ab_w_in, ab_w_out, lru_conv_w, lru_conv_b, lru_w_a, lru_b_a, lru_w_x, lru_b_x, lru_lambda, swa_sink, c_w_in, c_w_out, c_q_norm, c_k_norm, ffn2_norm, ffn2_w_gu, ffn2_w_down, final_norm):
    bsz, n, d = x.shape
    assert (bsz, n, d) == (1, SEQ, D_MODEL)
    cos, sin = _rope_tables(T_TOK, KEYS_PAD)
    h = jnp.concatenate([meta_tokens.astype(x.dtype), x[0]], axis=0)
    xg, rstd = norm_stats(h, ffn1_norm[0])
    w_gu1 = ffn1_w_gu[0].astype(BF16)
    for layer in range(DEPTH):
        i = layer // 2
        if layer % 2 == 0:
            in_job, out_job = (ab_w_in, i, 128, AB_IN // 2), (ab_w_out, i, 64, D_MODEL)
        else:
            in_job, out_job = (c_w_in, i, 128, C_IN // 3), (c_w_out, i, 64, D_MODEL)
        act, (w_dn1, w_in) = matmul_swiglu(xg, rstd, w_gu1,
                                           cast_jobs=[(ffn1_w_down, layer, 64, D_MODEL), in_job])
        (h, xg, rstd), (w_out,) = matmul_residual([act], w_dn1, h, 0.5, g_next=mix_norm[layer],
                                                  cast_jobs=[out_job])
        if layer % 2 == 0:
            ys = _mixer_ab(xg, rstd, i, w_in, lru_conv_w, lru_conv_b, lru_w_a, lru_b_a,
                           lru_w_x, lru_b_x, lru_lambda, swa_sink)
        else:
            ys = _mixer_c(xg, rstd, i, w_in, c_q_norm, c_k_norm, cos, sin)
        (h, xg, rstd), (w_gu2,) = matmul_residual(ys, w_out, h, 1.0, g_next=ffn2_norm[layer],
                                                  cast_jobs=[(ffn2_w_gu, layer, 64, 2 * D_FF)])
        last = layer + 1 == DEPTH
        jobs = [(ffn2_w_down, layer, 64, D_MODEL)]
        if not last:
            jobs.append((ffn1_w_gu, layer + 1, 128, D_MODEL))
        act, cast = matmul_swiglu(xg, rstd, w_gu2, cast_jobs=jobs)
        if last:
            h, _ = matmul_residual([act], cast[0], h, 0.5)
        else:
            w_gu1 = cast[1]
            (h, xg, rstd), _ = matmul_residual([act], cast[0], h, 0.5, g_next=ffn1_norm[layer + 1])
    return final_rmsnorm(h, final_norm)[None]
```

```python
import functools

import numpy as np
import jax
import jax.numpy as jnp
from jax import lax
from jax.experimental import pallas as pl
from jax.experimental.pallas import tpu as pltpu

F32 = jnp.float32
BF16 = jnp.bfloat16

D_MODEL = 4096
SEQ = 8192
DEPTH = 4
HEAD_DIM = 128
N_META = 16
T_TOK = SEQ + N_META
GRID_W = 64
BLOCK = 128
WINDOW = 128
RMS_EPS = 1e-6
NEG_INF = -1e30
D_FF = (3 * D_MODEL) // 2
LRU_WIDTH = D_MODEL // 2
LRU_C = 8.0
SWA_Q_HEADS = 16
SWA_KV_HEADS = 4
SWA_Q_DIM = SWA_Q_HEADS * HEAD_DIM
SWA_KV_DIM = SWA_KV_HEADS * HEAD_DIM
GA_Q_HEADS = 32
GA_KV_HEADS = 8
GA_Q_DIM = GA_Q_HEADS * HEAD_DIM
GA_KV_DIM = GA_KV_HEADS * HEAD_DIM
GQA_GROUP = 4
AB_IN = 2 * LRU_WIDTH + SWA_Q_DIM + 2 * SWA_KV_DIM
C_IN = GA_Q_DIM + 2 * GA_KV_DIM
ROPE_BASE = 10000.0
ATTN_SCALE = HEAD_DIM ** -0.5

V7X_VMEM_BYTES = 64 * 2**20
VMEM_CAP_BYTES = V7X_VMEM_BYTES - 6 * 2**20

TM = 912
LANES = 128
MXU_DIM = 256
KEYS_PAD = 33 * MXU_DIM
DENSE_TK = MXU_DIM
DENSE_TQ = 912
DENSE_HEADS = 2
PREP_TM = 384
NORM_TM = 432
LRU_L = 432
EXT = 2 * HEAD_DIM
EXP2_SCALE = ATTN_SCALE * float(np.log2(np.e))
PAD_KEY_PENALTY = 30000.0
FAST_SOFTMAX_MAX_LOGIT = 40.0


def _nbytes(shape, dtype):
    return int(np.prod(shape)) * jnp.dtype(dtype).itemsize


def _params(block_bytes, extra_bytes=0):
    need = 2 * block_bytes + extra_bytes + 4 * 2**20
    return pltpu.CompilerParams(vmem_limit_bytes=int(min(max(need, 16 * 2**20), VMEM_CAP_BYTES)))


def _embed_stats_kernel(meta_ref, tail_ref, main_ref, g_ref, h_ref, xg_ref, rstd_ref):
    tm = h_ref.shape[0]
    head = jnp.where(pl.program_id(0) == 0, meta_ref[...], tail_ref[...])
    x = jnp.concatenate([head, main_ref[:tm - N_META, :]], axis=0)
    h_ref[...] = x
    xg_ref[...] = (x * g_ref[...]).astype(xg_ref.dtype)
    rstd = lax.rsqrt(jnp.mean(x * x, axis=-1, keepdims=True) + RMS_EPS)
    rstd_ref[...] = jnp.broadcast_to(rstd, rstd_ref.shape)


def embed_stats(x, meta, g, tm=NORM_TM):
    n, d = x.shape
    m = n + N_META
    assert tm % N_META == 0 and m % tm == 0
    per = tm // N_META
    blk = (2 * _nbytes((tm, d), F32) + 2 * _nbytes((N_META, d), F32) + _nbytes((tm, d), BF16)
           + _nbytes((tm, LANES), F32))
    return pl.pallas_call(
        _embed_stats_kernel,
        grid=(m // tm,),
        in_specs=[pl.BlockSpec((N_META, d), lambda i: (0, 0)),
                  pl.BlockSpec((N_META, d), lambda i: (jnp.maximum(i * per - 1, 0), 0)),
                  pl.BlockSpec((tm, d), lambda i: (i, 0)),
                  pl.BlockSpec((1, d), lambda i: (0, 0))],
        out_specs=[pl.BlockSpec((tm, d), lambda i: (i, 0)),
                   pl.BlockSpec((tm, d), lambda i: (i, 0)),
                   pl.BlockSpec((tm, LANES), lambda i: (i, 0))],
        out_shape=[jax.ShapeDtypeStruct((m, d), F32),
                   jax.ShapeDtypeStruct((m, d), BF16),
                   jax.ShapeDtypeStruct((m, LANES), F32)],
        compiler_params=_params(blk, _nbytes((tm, d), F32)),
        name="embed_stats",
    )(meta, x, x, g.reshape(1, d))


def _final_norm_kernel(a_ref, b_ref, g_ref, o_ref):
    x = jnp.concatenate([a_ref[N_META:, :], b_ref[...]], axis=0)
    y = x * lax.rsqrt(jnp.mean(x * x, axis=-1, keepdims=True) + RMS_EPS)
    o_ref[...] = y * g_ref[...]


def final_rmsnorm(h, g, tm=256):
    m, d = h.shape
    n = m - N_META
    blk = 2 * _nbytes((tm, d), F32) + _nbytes((N_META, d), F32)
    return pl.pallas_call(
        _final_norm_kernel,
        grid=(n // tm,),
        in_specs=[pl.BlockSpec((tm, d), lambda i: (i, 0)),
                  pl.BlockSpec((N_META, d), lambda i: ((i + 1) * (tm // N_META), 0)),
                  pl.BlockSpec((1, d), lambda i: (0, 0))],
        out_specs=pl.BlockSpec((tm, d), lambda i: (i, 0)),
        out_shape=jax.ShapeDtypeStruct((n, d), F32),
        compiler_params=_params(blk, 2 * _nbytes((tm, d), F32)),
        name="final_norm",
    )(h, h, g.reshape(1, d))


def _dot(a, b):
    return jnp.dot(a, b, preferred_element_type=F32)


def _lane_tile(v, width):
    return jnp.concatenate([v] * (width // LANES), axis=1)


def _cast_specs(jobs, n_col_tiles, steps):
    in_specs, out_specs, out_shape, args, nbytes = [], [], [], [], 0
    for w, layer, rb, cb in jobs:
        _, k, n = w.shape
        assert k % rb == 0 and n % cb == 0 and (k // rb) * (n // cb) <= steps
        nc = n // cb
        last = (k // rb) * nc - 1

        def slab(i, j, nc=nc, last=last):
            t = jnp.minimum(i * n_col_tiles + j, last)
            return t // nc, t % nc

        in_specs.append(pl.BlockSpec((None, rb, cb),
                                     functools.partial(lambda i, j, f, l: (l, *f(i, j)), f=slab, l=layer)))
        out_specs.append(pl.BlockSpec((rb, cb), slab))
        out_shape.append(jax.ShapeDtypeStruct((k, n), BF16))
        args.append(w)
        nbytes += _nbytes((rb, cb), F32) + _nbytes((rb, cb), BF16)
    return in_specs, out_specs, out_shape, args, nbytes


def _run_cast_jobs(src_refs, dst_refs):
    for src, dst in zip(src_refs, dst_refs):
        dst[...] = src[...].astype(dst.dtype)


def _mm_kernel(*refs, n_cast):
    x_ref, rstd_ref, w_ref = refs[:3]
    o_ref = refs[3 + n_cast]
    _run_cast_jobs(refs[3:3 + n_cast], refs[4 + n_cast:])
    acc = _dot(x_ref[...], w_ref[...])
    o_ref[...] = (acc * _lane_tile(rstd_ref[...], acc.shape[1])).astype(o_ref.dtype)


def matmul(xg, rstd, w, n_off, n, out_dtype, tn, cast_jobs=(), tm=TM):
    m, k = xg.shape
    joff = n_off // tn
    grid = (m // tm, n // tn)
    c_in, c_out, c_shape, c_args, c_bytes = _cast_specs(cast_jobs, grid[1], grid[0] * grid[1])
    blk = (_nbytes((tm, k), BF16) + _nbytes((tm, LANES), F32) + _nbytes((k, tn), BF16)
           + _nbytes((tm, tn), out_dtype) + c_bytes)
    out = pl.pallas_call(
        functools.partial(_mm_kernel, n_cast=len(cast_jobs)),
        grid=grid,
        in_specs=[pl.BlockSpec((tm, k), lambda i, j: (i, 0)),
                  pl.BlockSpec((tm, LANES), lambda i, j: (i, 0)),
                  pl.BlockSpec((k, tn), lambda i, j: (0, j + joff)), *c_in],
        out_specs=[pl.BlockSpec((tm, tn), lambda i, j: (i, j)), *c_out],
        out_shape=[jax.ShapeDtypeStruct((m, n), out_dtype), *c_shape],
        compiler_params=_params(blk, 2 * _nbytes((tm, tn), F32)),
        name="matmul",
    )(xg, rstd, w, *c_args)
    return out[0], out[1:]


def _mm_swiglu_kernel(*refs, n_cast):
    x_ref, rstd_ref, wg_ref, wu_ref = refs[:4]
    o_ref = refs[4 + n_cast]
    _run_cast_jobs(refs[4:4 + n_cast], refs[5 + n_cast:])
    x = x_ref[...]
    rstd = _lane_tile(rstd_ref[...], o_ref.shape[1])
    gate = _dot(x, wg_ref[...]) * rstd
    up = _dot(x, wu_ref[...]) * rstd
    o_ref[...] = (jax.nn.silu(gate) * up).astype(o_ref.dtype)


def matmul_swiglu(xg, rstd, w_gu, cast_jobs=(), tn=512, tm=TM):
    m, k = xg.shape
    f = w_gu.shape[1] // 2
    uoff = f // tn
    grid = (m // tm, f // tn)
    c_in, c_out, c_shape, c_args, c_bytes = _cast_specs(cast_jobs, grid[1], grid[0] * grid[1])
    blk = (_nbytes((tm, k), BF16) + _nbytes((tm, LANES), F32) + 2 * _nbytes((k, tn), BF16)
           + _nbytes((tm, tn), BF16) + c_bytes)
    out = pl.pallas_call(
        functools.partial(_mm_swiglu_kernel, n_cast=len(cast_jobs)),
        grid=grid,
        in_specs=[pl.BlockSpec((tm, k), lambda i, j: (i, 0)),
                  pl.BlockSpec((tm, LANES), lambda i, j: (i, 0)),
                  pl.BlockSpec((k, tn), lambda i, j: (0, j)),
                  pl.BlockSpec((k, tn), lambda i, j: (0, j + uoff)), *c_in],
        out_specs=[pl.BlockSpec((tm, tn), lambda i, j: (i, j)), *c_out],
        out_shape=[jax.ShapeDtypeStruct((m, f), BF16), *c_shape],
        compiler_params=_params(blk, 3 * _nbytes((tm, tn), F32)),
        name="matmul_swiglu",
    )(xg, rstd, w_gu, w_gu, *c_args)
    return out[0], out[1:]


def _mm_res_kernel(*refs, n_x, n_cast, scale, with_stats, d_model):
    x_refs, w_refs = refs[:n_x], refs[n_x:2 * n_x]
    r_ref = refs[2 * n_x]
    n_in = 2 * n_x + 1 + int(with_stats)
    n_out = 3 if with_stats else 1
    outs = refs[n_in + n_cast:n_in + n_cast + n_out]
    _run_cast_jobs(refs[n_in:n_in + n_cast], refs[n_in + n_cast + n_out:n_in + 2 * n_cast + n_out])
    acc = _dot(x_refs[0][...], w_refs[0][...])
    for x_ref, w_ref in zip(x_refs[1:], w_refs[1:]):
        acc = acc + _dot(x_ref[...], w_ref[...])
    h = r_ref[...] + scale * acc
    if not with_stats:
        outs[0][...] = h
        return
    g_ref, ssq_ref = refs[n_in - 1], refs[-1]
    o_ref, xg_ref, rstd_ref = outs
    o_ref[...] = h
    xg_ref[...] = (h * g_ref[...]).astype(xg_ref.dtype)
    sq = h * h
    part = sq[:, 0:LANES]
    for c in range(1, sq.shape[1] // LANES):
        part = part + sq[:, c * LANES:(c + 1) * LANES]
    j = pl.program_id(1)

    @pl.when(j == 0)
    def _():
        ssq_ref[...] = part

    @pl.when(j > 0)
    def _():
        ssq_ref[...] += part

    @pl.when(j == pl.num_programs(1) - 1)
    def _():
        tot = jnp.sum(ssq_ref[...], axis=-1, keepdims=True)
        rstd_ref[...] = jnp.broadcast_to(lax.rsqrt(tot * (1.0 / d_model) + RMS_EPS), rstd_ref.shape)


def matmul_residual(xs, w, res, scale, g_next=None, cast_jobs=(), tn=512, tm=TM):
    m, k = xs[0].shape
    n_x = len(xs)
    assert all(x.shape == (m, k) for x in xs)
    n = w.shape[1]
    with_stats = g_next is not None
    grid = (m // tm, n // tn)
    c_in, c_out, c_shape, c_args, c_bytes = _cast_specs(cast_jobs, grid[1], grid[0] * grid[1])
    blk = (n_x * (_nbytes((tm, k), BF16) + _nbytes((k, tn), BF16)) + 2 * _nbytes((tm, tn), F32)
           + _nbytes((tm, tn), BF16) + _nbytes((tm, LANES), F32) + c_bytes)
    in_specs = [pl.BlockSpec((tm, k), lambda i, j: (i, 0)) for _ in xs]
    in_specs += [pl.BlockSpec((k, tn), functools.partial(lambda i, j, c: (c, j), c=c))
                 for c in range(n_x)]
    in_specs.append(pl.BlockSpec((tm, tn), lambda i, j: (i, j)))
    out_specs = [pl.BlockSpec((tm, tn), lambda i, j: (i, j))]
    out_shape = [jax.ShapeDtypeStruct((m, n), F32)]
    args = [*xs, *([w] * n_x), res]
    scratch = []
    if with_stats:
        in_specs.append(pl.BlockSpec((1, tn), lambda i, j: (0, j)))
        out_specs += [pl.BlockSpec((tm, tn), lambda i, j: (i, j)),
                      pl.BlockSpec((tm, LANES), lambda i, j: (i, 0))]
        out_shape += [jax.ShapeDtypeStruct((m, n), BF16), jax.ShapeDtypeStruct((m, LANES), F32)]
        args.append(g_next.reshape(1, n))
        scratch = [pltpu.VMEM((tm, LANES), F32)]
    n_main = len(out_shape)
    out = pl.pallas_call(
        functools.partial(_mm_res_kernel, n_x=n_x, n_cast=len(cast_jobs), scale=scale,
                          with_stats=with_stats, d_model=n),
        grid=grid,
        in_specs=[*in_specs, *c_in],
        out_specs=[*out_specs, *c_out],
        out_shape=[*out_shape, *c_shape],
        scratch_shapes=scratch,
        compiler_params=_params(blk, 3 * _nbytes((tm, tn), F32)),
        name="matmul_residual",
    )(*args, *c_args)
    main = out[:n_main]
    return (main if with_stats else main[0]), out[n_main:]


def _softplus(z):
    return jnp.maximum(z, 0.0) + jnp.log1p(jnp.exp(-jnp.abs(z)))


def _lru_kernel(u_ref, g_ref, cw_ref, cb_ref, wa_ref, ba_ref, wx_ref, bx_ref, lam_ref,
                o_ref, uc_ref, h_ref, *, chunk):
    t_len = u_ref.shape[0]
    n_chunks = t_len // chunk
    n_blk = chunk // 8
    cw = cw_ref[...]
    cb = cb_ref[...]
    row8 = lax.broadcasted_iota(jnp.int32, (8, HEAD_DIM), 0)

    def coeffs(uc, d):
        ucb = uc.astype(BF16)
        gate_r = _dot(ucb, wa_ref[d]) + ba_ref[d:d + 1, :]
        gate_i = _dot(ucb, wx_ref[d]) + bx_ref[d:d + 1, :]
        r = jax.nn.sigmoid(gate_r)
        i = 0.5 + 0.5 * jnp.tanh(0.5 * gate_i)
        log_a = (-LRU_C * r) * _softplus(-lam_ref[d:d + 1, :])
        a = jnp.exp(log_a)
        th = jnp.tanh(log_a)
        n = -2.0 * th
        coef = jnp.where(n > 0.0, n * lax.rsqrt(n * (1.0 - th)), 0.0)
        b = coef * (i * uc)
        return a, b

    def tile_scan(a, b, reverse):
        for s in (1, 2, 4):
            if reverse:
                keep = row8 < 8 - s
                shift = 8 - s
            else:
                keep = row8 >= s
                shift = s
            a_sh = jnp.where(keep, pltpu.roll(a, shift, 0), 1.0)
            b_sh = jnp.where(keep, pltpu.roll(b, shift, 0), 0.0)
            b = a * b_sh + b
            a = a * a_sh
        return a, b

    def fwd_chunk(c, carry):
        c0 = pl.multiple_of(c * chunk, 8)
        main = u_ref[pl.ds(c0, chunk), :]
        prev = u_ref[pl.ds(jnp.maximum(c0 - 8, 0), 8), :]
        prev = jnp.where(c > 0, prev, 0.0)
        nxt = u_ref[pl.ds(jnp.minimum(c0 + chunk, t_len - 8), 8), :]
        nxt = jnp.where(c < n_chunks - 1, nxt, 0.0)
        x = jnp.concatenate([prev, main, nxt], axis=0)
        uc = (x[6:6 + chunk] * cw[0:1] + x[7:7 + chunk] * cw[1:2]
              + x[8:8 + chunk] * cw[2:3] + x[9:9 + chunk] * cw[3:4]) + cb
        uc_ref[pl.ds(c0, chunk), :] = uc
        a, b = coeffs(uc, 0)
        for k in range(n_blk):
            at, bt = tile_scan(a[8 * k:8 * k + 8], b[8 * k:8 * k + 8], False)
            ht = at * carry + bt
            h_ref[pl.ds(c0 + 8 * k, 8), :] = ht
            carry = jnp.broadcast_to(ht[7:8, :], (8, HEAD_DIM))
        return carry

    lax.fori_loop(0, n_chunks, fwd_chunk, jnp.zeros((8, HEAD_DIM), F32))

    def rev_chunk(ci, carry):
        c = n_chunks - 1 - ci
        c0 = pl.multiple_of(c * chunk, 8)
        uc = uc_ref[pl.ds(c0, chunk), :]
        a, b = coeffs(uc, 1)
        for k in reversed(range(n_blk)):
            at, bt = tile_scan(a[8 * k:8 * k + 8], b[8 * k:8 * k + 8], True)
            ht = at * carry + bt
            rows = pl.ds(c0 + 8 * k, 8)
            h_ref[rows, :] = h_ref[rows, :] + ht
            carry = jnp.broadcast_to(ht[0:1, :], (8, HEAD_DIM))
        rows = pl.ds(c0, chunk)
        o_ref[rows, :] = (h_ref[rows, :] * jax.nn.gelu(g_ref[rows, :])).astype(o_ref.dtype)
        return carry

    lax.fori_loop(0, n_chunks, rev_chunk, jnp.zeros((8, HEAD_DIM), F32))


def rglru(ug, conv_w, conv_b, w_a, b_a, w_x, b_x, lam, chunk=LRU_L):
    t_len = ug.shape[0]
    width = ug.shape[1] // 2
    nb = width // HEAD_DIM
    c = HEAD_DIM
    blk = (2 * _nbytes((t_len, c), F32) + _nbytes((t_len, c), BF16)
           + 4 * _nbytes((c, c), BF16))
    scratch = 2 * _nbytes((t_len, c), F32)
    return pl.pallas_call(
        functools.partial(_lru_kernel, chunk=chunk),
        grid=(nb,),
        in_specs=[pl.BlockSpec((t_len, c), lambda n: (0, n)),
                  pl.BlockSpec((t_len, c), lambda n: (0, n + nb)),
                  pl.BlockSpec((4, c), lambda n: (0, n)),
                  pl.BlockSpec((1, c), lambda n: (0, n)),
                  pl.BlockSpec((2, None, c, c), lambda n: (0, n, 0, 0)),
                  pl.BlockSpec((2, c), lambda n: (0, n)),
                  pl.BlockSpec((2, None, c, c), lambda n: (0, n, 0, 0)),
                  pl.BlockSpec((2, c), lambda n: (0, n)),
                  pl.BlockSpec((2, c), lambda n: (0, n))],
        out_specs=pl.BlockSpec((t_len, c), lambda n: (0, n)),
        out_shape=jax.ShapeDtypeStruct((t_len, width), BF16),
        scratch_shapes=[pltpu.VMEM((t_len, c), F32), pltpu.VMEM((t_len, c), F32)],
        compiler_params=_params(blk, scratch + 16 * _nbytes((chunk, c), F32)),
        name="rglru",
    )(ug, ug, conv_w, conv_b.reshape(1, width), w_a.astype(BF16), b_a, w_x.astype(BF16), b_x, lam)


_NT = (((1,), (1,)), ((), ()))


def _sink_softmax_pv(s, sink, v):
    d = HEAD_DIM
    fold = s[:, 0:LANES]
    for c in range(1, s.shape[1] // LANES):
        fold = jnp.maximum(fold, s[:, c * LANES:(c + 1) * LANES])
    m = jnp.maximum(jnp.max(fold, axis=1, keepdims=True), sink)
    p = jnp.exp(s - m).astype(BF16)
    pv = _dot(p, jnp.concatenate([v, jnp.ones_like(v)], axis=1))
    return pv[:, :d] / (pv[:, d:] + jnp.exp(sink - m))


def _swa_kernel(sink_ref, slope_ref, q_ref, k_ref, v_ref, o_ref):
    h = pl.program_id(0)
    n_real = q_ref.shape[0] - N_META
    n_blocks = n_real // BLOCK
    d = HEAD_DIM
    zpad = jnp.zeros((BLOCK - N_META, d), BF16)
    k_meta = jnp.concatenate([k_ref[0:N_META, :], zpad], axis=0)
    v_meta = jnp.concatenate([v_ref[0:N_META, :], zpad], axis=0)
    kw = 3 * BLOCK

    def per_head_rows(ref, rows):
        return jnp.concatenate([jnp.full((rows, 1), ref[GQA_GROUP * h + g], F32)
                                for g in range(GQA_GROUP)], axis=0)

    def stack_heads(x):
        return jnp.concatenate([x[:, g * d:(g + 1) * d] for g in range(GQA_GROUP)], axis=0)

    sink = per_head_rows(sink_ref, BLOCK)
    neg_slope = -per_head_rows(slope_ref, BLOCK)
    col = lax.broadcasted_iota(jnp.int32, (GQA_GROUP * BLOCK, BLOCK + kw), 1)
    rowq = lax.broadcasted_iota(jnp.int32, (GQA_GROUP * BLOCK, BLOCK + kw), 0) % BLOCK
    is_meta = col < N_META
    in_cols = col >= BLOCK
    col_off = col - BLOCK - rowq

    def bias_for(first_key_minus_first_query):
        rel = jnp.abs(col_off + first_key_minus_first_query)
        in_win = in_cols & (rel <= WINDOW)
        return jnp.where(is_meta, 0.0, jnp.where(in_win, neg_slope * rel.astype(F32), NEG_INF))

    def attend(j, ws, bias):
        r0, w0 = N_META + BLOCK * j, N_META + ws
        if not isinstance(j, int):
            r0, w0 = pl.multiple_of(r0, N_META), pl.multiple_of(w0, N_META)
        kcat = jnp.concatenate([k_meta, k_ref[pl.ds(w0, kw), :]], axis=0)
        vcat = jnp.concatenate([v_meta, v_ref[pl.ds(w0, kw), :]], axis=0)
        qs = stack_heads(q_ref[pl.ds(r0, BLOCK), :])
        s = lax.dot_general(qs, kcat, _NT, preferred_element_type=F32) * ATTN_SCALE + bias
        o = _sink_softmax_pv(s, sink, vcat)
        for g in range(GQA_GROUP):
            o_ref[pl.ds(r0, BLOCK), g * d:(g + 1) * d] = o[g * BLOCK:(g + 1) * BLOCK].astype(o_ref.dtype)

    attend(0, 0, bias_for(0))
    bias_mid = bias_for(-BLOCK)

    def block(j, carry):
        attend(j, BLOCK * (j - 1), bias_mid)
        return carry

    lax.fori_loop(1, n_blocks - 1, block, 0, unroll=4)
    attend(n_blocks - 1, n_real - kw, bias_for(n_real - kw - BLOCK * (n_blocks - 1)))

    kcat = jnp.concatenate([k_meta, k_ref[N_META:N_META + BLOCK, :]], axis=0)
    vcat = jnp.concatenate([v_meta, v_ref[N_META:N_META + BLOCK, :]], axis=0)
    colm = lax.broadcasted_iota(jnp.int32, (GQA_GROUP * N_META, 2 * BLOCK), 1)
    rowm = lax.broadcasted_iota(jnp.int32, (GQA_GROUP * N_META, 2 * BLOCK), 0) % N_META
    ok = (colm < N_META) | ((colm >= BLOCK) & (colm - BLOCK + N_META - rowm <= WINDOW))
    bias = jnp.where(ok, 0.0, NEG_INF)
    qs = stack_heads(q_ref[0:N_META, :])
    s = lax.dot_general(qs, kcat, _NT, preferred_element_type=F32) * ATTN_SCALE + bias
    o = _sink_softmax_pv(s, per_head_rows(sink_ref, N_META), vcat)
    for g in range(GQA_GROUP):
        o_ref[0:N_META, g * d:(g + 1) * d] = o[g * N_META:(g + 1) * N_META].astype(o_ref.dtype)


def windowed_attention(qkv, sink):
    t_len = qkv.shape[0]
    d = HEAD_DIM
    gw = GQA_GROUP * d
    slopes = jnp.asarray(2.0 ** (-8.0 * np.arange(1, SWA_Q_HEADS + 1) / SWA_Q_HEADS), F32)
    k_blk0 = SWA_Q_DIM // d
    v_blk0 = (SWA_Q_DIM + SWA_KV_DIM) // d
    blk = 2 * _nbytes((t_len, gw), BF16) + 2 * _nbytes((t_len, d), BF16)
    smem = pl.BlockSpec(memory_space=pltpu.SMEM)
    return pl.pallas_call(
        _swa_kernel,
        grid=(SWA_KV_HEADS,),
        in_specs=[smem, smem,
                  pl.BlockSpec((t_len, gw), lambda h: (0, h)),
                  pl.BlockSpec((t_len, d), lambda h: (0, k_blk0 + h)),
                  pl.BlockSpec((t_len, d), lambda h: (0, v_blk0 + h))],
        out_specs=pl.BlockSpec((t_len, gw), lambda h: (0, h)),
        out_shape=jax.ShapeDtypeStruct((t_len, SWA_Q_DIM), BF16),
        compiler_params=_params(blk, 8 * 2**20),
        name="windowed_attention",
    )(sink.astype(F32), slopes, qkv, qkv, qkv)


def _rope_tables(t_len, rows_pad):
    half = HEAD_DIM // 2
    n = t_len - N_META
    rows = n // GRID_W
    row = jnp.concatenate([jnp.zeros((N_META,), jnp.int32),
                           jnp.repeat(jnp.arange(rows, dtype=jnp.int32), GRID_W)])
    col = jnp.concatenate([jnp.zeros((N_META,), jnp.int32),
                           jnp.tile(jnp.arange(GRID_W, dtype=jnp.int32), rows)])
    inv_freq = ROPE_BASE ** (-(jnp.arange(half // 2, dtype=F32) * 2.0 / half))
    ang_row = row.astype(F32)[:, None] * inv_freq
    ang_col = col.astype(F32)[:, None] * inv_freq
    cos = jnp.concatenate([jnp.cos(ang_row), jnp.cos(ang_row), jnp.cos(ang_col), jnp.cos(ang_col)], axis=1)
    sin = jnp.concatenate([-jnp.sin(ang_row), jnp.sin(ang_row), -jnp.sin(ang_col), jnp.sin(ang_col)], axis=1)
    pad = ((0, rows_pad - t_len), (0, 0))
    return jnp.pad(cos, pad), jnp.pad(sin, pad)


def _qkv_prep_kernel(x_ref, cos_ref, sin_ref, qn_ref, kn_ref, q_ref, k_ref, v_ref, *, t_len):
    tm = x_ref.shape[0]
    d = HEAD_DIM
    cos = cos_ref[...]
    sin = sin_ref[...]
    lane = lax.broadcasted_iota(jnp.int32, (tm, d), 1)
    first = (lane % (d // 2)) < (d // 4)
    rows = pl.program_id(0) * tm + lax.broadcasted_iota(jnp.int32, (tm, 1), 0)
    valid = rows < t_len

    def norm_rope(xh, g):
        y = xh * lax.rsqrt(jnp.mean(xh * xh, axis=-1, keepdims=True) + RMS_EPS) * g
        partner = jnp.where(first, pltpu.roll(y, d - d // 4, 1), pltpu.roll(y, d // 4, 1))
        return y * cos + partner * sin

    qn = qn_ref[...]
    kn = kn_ref[...]
    for hd in range(GA_Q_HEADS):
        q_ref[:, hd * d:(hd + 1) * d] = norm_rope(x_ref[:, hd * d:(hd + 1) * d], qn).astype(q_ref.dtype)
    k_aux = jnp.where(lane == 0, 1.0, jnp.where((lane == 1) & jnp.logical_not(valid), 1.0, 0.0))
    ones = jnp.ones((tm, d), v_ref.dtype)
    for hd in range(GA_KV_HEADS):
        kk = norm_rope(x_ref[:, GA_Q_DIM + hd * d:GA_Q_DIM + (hd + 1) * d], kn)
        k_ref[:, hd * EXT:hd * EXT + d] = jnp.where(valid, kk, 0.0).astype(k_ref.dtype)
        k_ref[:, hd * EXT + d:(hd + 1) * EXT] = k_aux.astype(k_ref.dtype)
        v = x_ref[:, GA_Q_DIM + GA_KV_DIM + hd * d:GA_Q_DIM + GA_KV_DIM + (hd + 1) * d]
        v_ref[:, hd * EXT:hd * EXT + d] = jnp.where(valid, v, 0.0).astype(v_ref.dtype)
        v_ref[:, hd * EXT + d:(hd + 1) * EXT] = ones


def qkv_prep(qkv, q_norm, k_norm, cos, sin, keys_pad=KEYS_PAD, tm=PREP_TM):
    t_len, width = qkv.shape
    d = HEAD_DIM
    blk = (_nbytes((tm, width), F32) + 2 * _nbytes((tm, d), F32)
           + _nbytes((tm, GA_Q_DIM), BF16) + 2 * _nbytes((tm, GA_KV_HEADS * EXT), BF16))
    return pl.pallas_call(
        functools.partial(_qkv_prep_kernel, t_len=t_len),
        grid=(keys_pad // tm,),
        in_specs=[pl.BlockSpec((tm, width), lambda i: (i, 0)),
                  pl.BlockSpec((tm, d), lambda i: (i, 0)),
                  pl.BlockSpec((tm, d), lambda i: (i, 0)),
                  pl.BlockSpec((1, d), lambda i: (0, 0)),
                  pl.BlockSpec((1, d), lambda i: (0, 0))],
        out_specs=[pl.BlockSpec((tm, GA_Q_DIM), lambda i: (i, 0)),
                   pl.BlockSpec((tm, GA_KV_HEADS * EXT), lambda i: (i, 0)),
                   pl.BlockSpec((tm, GA_KV_HEADS * EXT), lambda i: (i, 0))],
        out_shape=[jax.ShapeDtypeStruct((t_len, GA_Q_DIM), BF16),
                   jax.ShapeDtypeStruct((keys_pad, GA_KV_HEADS * EXT), BF16),
                   jax.ShapeDtypeStruct((keys_pad, GA_KV_HEADS * EXT), BF16)],
        compiler_params=_params(blk, 4 * 2**20),
        name="qkv_prep",
    )(qkv, cos, sin, q_norm.reshape(1, d), k_norm.reshape(1, d))


def _stacked_q_ext(q_ref, shift_ref):
    d = HEAD_DIM
    q = jnp.concatenate([q_ref[:, g * d:(g + 1) * d] for g in range(q_ref.shape[1] // d)], axis=0)
    lane = lax.broadcasted_iota(jnp.int32, q.shape, 1)
    aux = jnp.where(lane == 0, -shift_ref[0], jnp.where(lane == 1, -PAD_KEY_PENALTY, 0.0))
    return jnp.concatenate([q, aux.astype(q.dtype)], axis=1)


def _store_heads(o_ref, acc):
    d = HEAD_DIM
    tq = o_ref.shape[0]
    o = acc[:, :d] / acc[:, d:]
    for g in range(o_ref.shape[1] // d):
        o_ref[:, g * d:(g + 1) * d] = o[g * tq:(g + 1) * tq].astype(o_ref.dtype)


def _dense_fast_kernel(shift_ref, q_ref, k_ref, v_ref, o_ref, *, tk):
    q = _stacked_q_ext(q_ref, shift_ref)
    acc = None
    for c in range(k_ref.shape[0] // tk):
        s = lax.dot_general(q, k_ref[c * tk:(c + 1) * tk, :], _NT, preferred_element_type=F32)
        p = jnp.exp2(s * EXP2_SCALE).astype(BF16)
        pv = _dot(p, v_ref[c * tk:(c + 1) * tk, :])
        acc = pv if acc is None else acc + pv
    _store_heads(o_ref, acc)


def _dense_safe_kernel(shift_ref, q_ref, k_ref, v_ref, o_ref, *, tk, n_keys):
    q = _stacked_q_ext(q_ref, shift_ref)
    rows = q.shape[0]

    def chunk(c, carry):
        m, acc = carry
        k0 = pl.multiple_of(c * tk, tk)
        t = lax.dot_general(q, k_ref[pl.ds(k0, tk), :], _NT, preferred_element_type=F32) * EXP2_SCALE
        key = k0 + lax.broadcasted_iota(jnp.int32, (rows, tk), 1)
        t = jnp.where(key < n_keys, t, NEG_INF)
        m_new = jnp.maximum(m, jnp.max(t, axis=1, keepdims=True))
        p = jnp.exp2(t - m_new).astype(BF16)
        acc = jnp.exp2(m - m_new) * acc + _dot(p, v_ref[pl.ds(k0, tk), :])
        return m_new, acc

    init = (jnp.full((rows, 1), NEG_INF, F32), jnp.zeros((rows, EXT), F32))
    _, acc = lax.fori_loop(0, k_ref.shape[0] // tk, chunk, init)
    _store_heads(o_ref, acc)


def dense_attention(q, k_ext, v_ext, shift, fast, tq=DENSE_TQ, tk=DENSE_TK, heads=DENSE_HEADS):
    t_len = q.shape[0]
    d = HEAD_DIM
    kp = k_ext.shape[0]
    assert GQA_GROUP % heads == 0
    groups_per_kv = GQA_GROUP // heads
    blk = 2 * _nbytes((tq, heads * d), BF16) + 2 * _nbytes((kp, EXT), BF16)
    if fast:
        body = functools.partial(_dense_fast_kernel, tk=tk)
    else:
        body = functools.partial(_dense_safe_kernel, tk=tk, n_keys=t_len)
    return pl.pallas_call(
        body,
        grid=(GA_Q_HEADS // heads, t_len // tq),
        in_specs=[pl.BlockSpec(memory_space=pltpu.SMEM),
                  pl.BlockSpec((tq, heads * d), lambda h, i: (i, h)),
                  pl.BlockSpec((kp, EXT), lambda h, i: (0, h // groups_per_kv)),
                  pl.BlockSpec((kp, EXT), lambda h, i: (0, h // groups_per_kv))],
        out_specs=pl.BlockSpec((tq, heads * d), lambda h, i: (i, h)),
        out_shape=jax.ShapeDtypeStruct((t_len, GA_Q_DIM), BF16),
        compiler_params=_params(blk, 10 * _nbytes((heads * tq, tk), F32)),
        name="dense_attention_fast" if fast else "dense_attention_safe",
    )(shift, q, k_ext, v_ext)


def _mixer_ab(xg, rstd, i, w_in, conv_w, conv_b, w_a, b_a, w_x, b_x, lam, sink):
    ug, _ = matmul(xg, rstd, w_in, 0, 2 * LRU_WIDTH, F32, tn=1024)
    qkv, _ = matmul(xg, rstd, w_in, 2 * LRU_WIDTH, SWA_Q_DIM + 2 * SWA_KV_DIM, BF16, tn=1024)
    y_a = rglru(ug, conv_w[i], conv_b[i], w_a[i], b_a[i], w_x[i], b_x[i], lam[i])
    y_b = windowed_attention(qkv, sink[i])
    return [y_a, y_b]


def _mixer_c(xg, rstd, i, w_in, q_norm, k_norm, cos, sin):
    qkv, _ = matmul(xg, rstd, w_in, 0, GA_Q_DIM + 2 * GA_KV_DIM, F32, tn=1024)
    q, k_ext, v_ext = qkv_prep(qkv, q_norm[i], k_norm[i], cos, sin)
    shift = (HEAD_DIM * jnp.max(jnp.abs(q_norm[i])) * jnp.max(jnp.abs(k_norm[i]))).reshape(1)
    o = lax.cond(ATTN_SCALE * shift[0] <= FAST_SOFTMAX_MAX_LOGIT,
                 lambda a: dense_attention(*a, fast=True),
                 lambda a: dense_attention(*a, fast=False), (q, k_ext, v_ext, shift))
    return [o]


def kernel(x, meta_tokens, ffn1_norm, ffn1_w_gu, ffn1_w_down, mix_norm, ab_w_in, ab_w_out, lru_conv_w, lru_conv_b, lru_w_a, lru_b_a, lru_w_x, lru_b_x, lru_lambda, swa_sink, c_w_in, c_w_out, c_q_norm, c_k_norm, ffn2_norm, ffn2_w_gu, ffn2_w_down, final_norm):
    bsz, n, d = x.shape
    assert (bsz, n, d) == (1, SEQ, D_MODEL)
    cos, sin = _rope_tables(T_TOK, KEYS_PAD)
    h, xg, rstd = embed_stats(x[0], meta_tokens.astype(x.dtype), ffn1_norm[0])
    w_gu1 = ffn1_w_gu[0].astype(BF16)
    for layer in range(DEPTH):
        i = layer // 2
        if layer % 2 == 0:
            in_job, out_job = (ab_w_in, i, 128, AB_IN // 2), (ab_w_out, i, 64, D_MODEL)
        else:
            in_job, out_job = (c_w_in, i, 128, C_IN // 3), (c_w_out, i, 64, D_MODEL)
        act, (w_dn1, w_in) = matmul_swiglu(xg, rstd, w_gu1,
                                           cast_jobs=[(ffn1_w_down, layer, 64, D_MODEL), in_job])
        (h, xg, rstd), (w_out,) = matmul_residual([act], w_dn1, h, 0.5, g_next=mix_norm[layer],
                                                  cast_jobs=[out_job])
        if layer % 2 == 0:
            ys = _mixer_ab(xg, rstd, i, w_in, lru_conv_w, lru_conv_b, lru_w_a, lru_b_a,
                           lru_w_x, lru_b_x, lru_lambda, swa_sink)
        else:
            ys = _mixer_c(xg, rstd, i, w_in, c_q_norm, c_k_norm, cos, sin)
        (h, xg, rstd), (w_gu2,) = matmul_residual(ys, w_out, h, 1.0, g_next=ffn2_norm[layer],
                                                  cast_jobs=[(ffn2_w_gu, layer, 64, 2 * D_FF)])
        last = layer + 1 == DEPTH
        jobs = [(ffn2_w_down, layer, 64, D_MODEL)]
        if not last:
            jobs.append((ffn1_w_gu, layer + 1, 128, D_MODEL))
        act, cast = matmul_swiglu(xg, rstd, w_gu2, cast_jobs=jobs)
        if last:
            h, _ = matmul_residual([act], cast[0], h, 0.5)
        else:
            w_gu1 = cast[1]
            (h, xg, rstd), _ = matmul_residual([act], cast[0], h, 0.5, g_next=ffn1_norm[layer + 1])
    return final_rmsnorm(h, final_norm)[None]
```

```python
import functools

import numpy as np
import jax
import jax.numpy as jnp
from jax import lax
from jax.experimental import pallas as pl
from jax.experimental.pallas import tpu as pltpu

F32 = jnp.float32
BF16 = jnp.bfloat16

D_MODEL = 4096
SEQ = 8192
DEPTH = 4
HEAD_DIM = 128
N_META = 16
T_TOK = SEQ + N_META
GRID_W = 64
BLOCK = 128
WINDOW = 128
RMS_EPS = 1e-6
NEG_INF = -1e30
D_FF = (3 * D_MODEL) // 2
LRU_WIDTH = D_MODEL // 2
LRU_C = 8.0
SWA_Q_HEADS = 16
SWA_KV_HEADS = 4
SWA_Q_DIM = SWA_Q_HEADS * HEAD_DIM
SWA_KV_DIM = SWA_KV_HEADS * HEAD_DIM
GA_Q_HEADS = 32
GA_KV_HEADS = 8
GA_Q_DIM = GA_Q_HEADS * HEAD_DIM
GA_KV_DIM = GA_KV_HEADS * HEAD_DIM
GQA_GROUP = 4
AB_IN = 2 * LRU_WIDTH + SWA_Q_DIM + 2 * SWA_KV_DIM
C_IN = GA_Q_DIM + 2 * GA_KV_DIM
ROPE_BASE = 10000.0
ATTN_SCALE = HEAD_DIM ** -0.5

V7X_VMEM_BYTES = 64 * 2**20
VMEM_CAP_BYTES = V7X_VMEM_BYTES - 6 * 2**20

TM = 912
LANES = 128
MXU_DIM = 256
KEYS_PAD = 33 * MXU_DIM
DENSE_TK = MXU_DIM
DENSE_TQ = 912
DENSE_HEADS = 2
PREP_TM = 384
NORM_TM = 432
LRU_L = 432
EXT = 2 * HEAD_DIM
EXP2_SCALE = ATTN_SCALE * float(np.log2(np.e))
PAD_KEY_PENALTY = 30000.0
FAST_SOFTMAX_MAX_LOGIT = 40.0


def _nbytes(shape, dtype):
    return int(np.prod(shape)) * jnp.dtype(dtype).itemsize


def _params(block_bytes, extra_bytes=0):
    need = 2 * block_bytes + extra_bytes + 4 * 2**20
    return pltpu.CompilerParams(vmem_limit_bytes=int(min(max(need, 16 * 2**20), VMEM_CAP_BYTES)))


def _embed_stats_kernel(meta_ref, tail_ref, main_ref, g_ref, h_ref, xg_ref, rstd_ref):
    tm = h_ref.shape[0]
    head = jnp.where(pl.program_id(0) == 0, meta_ref[...], tail_ref[...])
    x = jnp.concatenate([head, main_ref[:tm - N_META, :]], axis=0)
    h_ref[...] = x
    xg_ref[...] = (x * g_ref[...]).astype(xg_ref.dtype)
    rstd = lax.rsqrt(jnp.mean(x * x, axis=-1, keepdims=True) + RMS_EPS)
    rstd_ref[...] = jnp.broadcast_to(rstd, rstd_ref.shape)


def embed_stats(x, meta, g, tm=NORM_TM):
    n, d = x.shape
    m = n + N_META
    assert tm % N_META == 0 and m % tm == 0
    per = tm // N_META
    blk = (2 * _nbytes((tm, d), F32) + 2 * _nbytes((N_META, d), F32) + _nbytes((tm, d), BF16)
           + _nbytes((tm, LANES), F32))
    return pl.pallas_call(
        _embed_stats_kernel,
        grid=(m // tm,),
        in_specs=[pl.BlockSpec((N_META, d), lambda i: (0, 0)),
                  pl.BlockSpec((N_META, d), lambda i: (jnp.maximum(i * per - 1, 0), 0)),
                  pl.BlockSpec((tm, d), lambda i: (i, 0)),
                  pl.BlockSpec((1, d), lambda i: (0, 0))],
        out_specs=[pl.BlockSpec((tm, d), lambda i: (i, 0)),
                   pl.BlockSpec((tm, d), lambda i: (i, 0)),
                   pl.BlockSpec((tm, LANES), lambda i: (i, 0))],
        out_shape=[jax.ShapeDtypeStruct((m, d), F32),
                   jax.ShapeDtypeStruct((m, d), BF16),
                   jax.ShapeDtypeStruct((m, LANES), F32)],
        compiler_params=_params(blk, _nbytes((tm, d), F32)),
        name="embed_stats",
    )(meta, x, x, g.reshape(1, d))


def _final_norm_kernel(a_ref, b_ref, g_ref, o_ref):
    x = jnp.concatenate([a_ref[N_META:, :], b_ref[...]], axis=0)
    y = x * lax.rsqrt(jnp.mean(x * x, axis=-1, keepdims=True) + RMS_EPS)
    o_ref[...] = y * g_ref[...]


def final_rmsnorm(h, g, tm=256):
    m, d = h.shape
    n = m - N_META
    blk = 2 * _nbytes((tm, d), F32) + _nbytes((N_META, d), F32)
    return pl.pallas_call(
        _final_norm_kernel,
        grid=(n // tm,),
        in_specs=[pl.BlockSpec((tm, d), lambda i: (i, 0)),
                  pl.BlockSpec((N_META, d), lambda i: ((i + 1) * (tm // N_META), 0)),
                  pl.BlockSpec((1, d), lambda i: (0, 0))],
        out_specs=pl.BlockSpec((tm, d), lambda i: (i, 0)),
        out_shape=jax.ShapeDtypeStruct((n, d), F32),
        compiler_params=_params(blk, 2 * _nbytes((tm, d), F32)),
        name="final_norm",
    )(h, h, g.reshape(1, d))


def _dot(a, b):
    return jnp.dot(a, b, preferred_element_type=F32)


def _lane_tile(v, width):
    return jnp.concatenate([v] * (width // LANES), axis=1)


def _cast_specs(jobs, n_col_tiles, steps):
    in_specs, out_specs, out_shape, args, nbytes = [], [], [], [], 0
    for w, layer, rb, cb in jobs:
        _, k, n = w.shape
        assert k % rb == 0 and n % cb == 0 and (k // rb) * (n // cb) <= steps
        nc = n // cb
        last = (k // rb) * nc - 1

        def slab(i, j, nc=nc, last=last):
            t = jnp.minimum(i * n_col_tiles + j, last)
            return t // nc, t % nc

        in_specs.append(pl.BlockSpec((None, rb, cb),
                                     functools.partial(lambda i, j, f, l: (l, *f(i, j)), f=slab, l=layer)))
        out_specs.append(pl.BlockSpec((rb, cb), slab))
        out_shape.append(jax.ShapeDtypeStruct((k, n), BF16))
        args.append(w)
        nbytes += _nbytes((rb, cb), F32) + _nbytes((rb, cb), BF16)
    return in_specs, out_specs, out_shape, args, nbytes


def _run_cast_jobs(src_refs, dst_refs):
    for src, dst in zip(src_refs, dst_refs):
        dst[...] = src[...].astype(dst.dtype)


def _mm_kernel(*refs, n_cast):
    x_ref, rstd_ref, w_ref = refs[:3]
    o_ref = refs[3 + n_cast]
    _run_cast_jobs(refs[3:3 + n_cast], refs[4 + n_cast:])
    acc = _dot(x_ref[...], w_ref[...])
    o_ref[...] = (acc * _lane_tile(rstd_ref[...], acc.shape[1])).astype(o_ref.dtype)


def matmul(xg, rstd, w, n_off, n, out_dtype, tn, cast_jobs=(), tm=TM):
    m, k = xg.shape
    joff = n_off // tn
    grid = (m // tm, n // tn)
    c_in, c_out, c_shape, c_args, c_bytes = _cast_specs(cast_jobs, grid[1], grid[0] * grid[1])
    blk = (_nbytes((tm, k), BF16) + _nbytes((tm, LANES), F32) + _nbytes((k, tn), BF16)
           + _nbytes((tm, tn), out_dtype) + c_bytes)
    out = pl.pallas_call(
        functools.partial(_mm_kernel, n_cast=len(cast_jobs)),
        grid=grid,
        in_specs=[pl.BlockSpec((tm, k), lambda i, j: (i, 0)),
                  pl.BlockSpec((tm, LANES), lambda i, j: (i, 0)),
                  pl.BlockSpec((k, tn), lambda i, j: (0, j + joff)), *c_in],
        out_specs=[pl.BlockSpec((tm, tn), lambda i, j: (i, j)), *c_out],
        out_shape=[jax.ShapeDtypeStruct((m, n), out_dtype), *c_shape],
        compiler_params=_params(blk, 2 * _nbytes((tm, tn), F32)),
        name="matmul",
    )(xg, rstd, w, *c_args)
    return out[0], out[1:]


def _mm_swiglu_kernel(*refs, n_cast):
    x_ref, rstd_ref, wg_ref, wu_ref = refs[:4]
    o_ref = refs[4 + n_cast]
    _run_cast_jobs(refs[4:4 + n_cast], refs[5 + n_cast:])
    x = x_ref[...]
    rstd = _lane_tile(rstd_ref[...], o_ref.shape[1])
    gate = _dot(x, wg_ref[...]) * rstd
    up = _dot(x, wu_ref[...]) * rstd
    o_ref[...] = (jax.nn.silu(gate) * up).astype(o_ref.dtype)


def matmul_swiglu(xg, rstd, w_gu, cast_jobs=(), tn=512, tm=TM):
    m, k = xg.shape
    f = w_gu.shape[1] // 2
    uoff = f // tn
    grid = (m // tm, f // tn)
    c_in, c_out, c_shape, c_args, c_bytes = _cast_specs(cast_jobs, grid[1], grid[0] * grid[1])
    blk = (_nbytes((tm, k), BF16) + _nbytes((tm, LANES), F32) + 2 * _nbytes((k, tn), BF16)
           + _nbytes((tm, tn), BF16) + c_bytes)
    out = pl.pallas_call(
        functools.partial(_mm_swiglu_kernel, n_cast=len(cast_jobs)),
        grid=grid,
        in_specs=[pl.BlockSpec((tm, k), lambda i, j: (i, 0)),
                  pl.BlockSpec((tm, LANES), lambda i, j: (i, 0)),
                  pl.BlockSpec((k, tn), lambda i, j: (0, j)),
                  pl.BlockSpec((k, tn), lambda i, j: (0, j + uoff)), *c_in],
        out_specs=[pl.BlockSpec((tm, tn), lambda i, j: (i, j)), *c_out],
        out_shape=[jax.ShapeDtypeStruct((m, f), BF16), *c_shape],
        compiler_params=_params(blk, 3 * _nbytes((tm, tn), F32)),
        name="matmul_swiglu",
    )(xg, rstd, w_gu, w_gu, *c_args)
    return out[0], out[1:]


def _mm_res_kernel(*refs, n_x, n_cast, scale, with_stats, d_model):
    x_refs, w_refs = refs[:n_x], refs[n_x:2 * n_x]
    r_ref = refs[2 * n_x]
    n_in = 2 * n_x + 1 + int(with_stats)
    n_out = 3 if with_stats else 1
    outs = refs[n_in + n_cast:n_in + n_cast + n_out]
    _run_cast_jobs(refs[n_in:n_in + n_cast], refs[n_in + n_cast + n_out:n_in + 2 * n_cast + n_out])
    acc = _dot(x_refs[0][...], w_refs[0][...])
    for x_ref, w_ref in zip(x_refs[1:], w_refs[1:]):
        acc = acc + _dot(x_ref[...], w_ref[...])
    h = r_ref[...] + scale * acc
    if not with_stats:
        outs[0][...] = h
        return
    g_ref, ssq_ref = refs[n_in - 1], refs[-1]
    o_ref, xg_ref, rstd_ref = outs
    o_ref[...] = h
    xg_ref[...] = (h * g_ref[...]).astype(xg_ref.dtype)
    sq = h * h
    part = sq[:, 0:LANES]
    for c in range(1, sq.shape[1] // LANES):
        part = part + sq[:, c * LANES:(c + 1) * LANES]
    j = pl.program_id(1)

    @pl.when(j == 0)
    def _():
        ssq_ref[...] = part

    @pl.when(j > 0)
    def _():
        ssq_ref[...] += part

    @pl.when(j == pl.num_programs(1) - 1)
    def _():
        tot = jnp.sum(ssq_ref[...], axis=-1, keepdims=True)
        rstd_ref[...] = jnp.broadcast_to(lax.rsqrt(tot * (1.0 / d_model) + RMS_EPS), rstd_ref.shape)


def matmul_residual(xs, w, res, scale, g_next=None, cast_jobs=(), tn=512, tm=TM):
    m, k = xs[0].shape
    n_x = len(xs)
    assert all(x.shape == (m, k) for x in xs)
    n = w.shape[1]
    with_stats = g_next is not None
    grid = (m // tm, n // tn)
    c_in, c_out, c_shape, c_args, c_bytes = _cast_specs(cast_jobs, grid[1], grid[0] * grid[1])
    blk = (n_x * (_nbytes((tm, k), BF16) + _nbytes((k, tn), BF16)) + 2 * _nbytes((tm, tn), F32)
           + _nbytes((tm, tn), BF16) + _nbytes((tm, LANES), F32) + c_bytes)
    in_specs = [pl.BlockSpec((tm, k), lambda i, j: (i, 0)) for _ in xs]
    in_specs += [pl.BlockSpec((k, tn), functools.partial(lambda i, j, c: (c, j), c=c))
                 for c in range(n_x)]
    in_specs.append(pl.BlockSpec((tm, tn), lambda i, j: (i, j)))
    out_specs = [pl.BlockSpec((tm, tn), lambda i, j: (i, j))]
    out_shape = [jax.ShapeDtypeStruct((m, n), F32)]
    args = [*xs, *([w] * n_x), res]
    scratch = []
    if with_stats:
        in_specs.append(pl.BlockSpec((1, tn), lambda i, j: (0, j)))
        out_specs += [pl.BlockSpec((tm, tn), lambda i, j: (i, j)),
                      pl.BlockSpec((tm, LANES), lambda i, j: (i, 0))]
        out_shape += [jax.ShapeDtypeStruct((m, n), BF16), jax.ShapeDtypeStruct((m, LANES), F32)]
        args.append(g_next.reshape(1, n))
        scratch = [pltpu.VMEM((tm, LANES), F32)]
    n_main = len(out_shape)
    out = pl.pallas_call(
        functools.partial(_mm_res_kernel, n_x=n_x, n_cast=len(cast_jobs), scale=scale,
                          with_stats=with_stats, d_model=n),
        grid=grid,
        in_specs=[*in_specs, *c_in],
        out_specs=[*out_specs, *c_out],
        out_shape=[*out_shape, *c_shape],
        scratch_shapes=scratch,
        compiler_params=_params(blk, 3 * _nbytes((tm, tn), F32)),
        name="matmul_residual",
    )(*args, *c_args)
    main = out[:n_main]
    return (main if with_stats else main[0]), out[n_main:]


def _softplus(z):
    return jnp.maximum(z, 0.0) + jnp.log1p(jnp.exp(-jnp.abs(z)))


def _lru_kernel(u_ref, g_ref, cw_ref, cb_ref, wa_ref, ba_ref, wx_ref, bx_ref, lam_ref,
                o_ref, uc_ref, h_ref, *, chunk):
    t_len = u_ref.shape[0]
    n_chunks = t_len // chunk
    n_blk = chunk // 8
    cw = cw_ref[...]
    cb = cb_ref[...]
    row8 = lax.broadcasted_iota(jnp.int32, (8, HEAD_DIM), 0)

    def coeffs(uc, d):
        ucb = uc.astype(BF16)
        gate_r = _dot(ucb, wa_ref[d]) + ba_ref[d:d + 1, :]
        gate_i = _dot(ucb, wx_ref[d]) + bx_ref[d:d + 1, :]
        r = jax.nn.sigmoid(gate_r)
        i = 0.5 + 0.5 * jnp.tanh(0.5 * gate_i)
        log_a = (-LRU_C * r) * _softplus(-lam_ref[d:d + 1, :])
        a = jnp.exp(log_a)
        th = jnp.tanh(log_a)
        n = -2.0 * th
        coef = jnp.where(n > 0.0, n * lax.rsqrt(n * (1.0 - th)), 0.0)
        b = coef * (i * uc)
        return a, b

    def tile_scan(a, b, reverse):
        for s in (1, 2, 4):
            if reverse:
                keep = row8 < 8 - s
                shift = 8 - s
            else:
                keep = row8 >= s
                shift = s
            a_sh = jnp.where(keep, pltpu.roll(a, shift, 0), 1.0)
            b_sh = jnp.where(keep, pltpu.roll(b, shift, 0), 0.0)
            b = a * b_sh + b
            a = a * a_sh
        return a, b

    def fwd_chunk(c, carry):
        c0 = pl.multiple_of(c * chunk, 8)
        main = u_ref[pl.ds(c0, chunk), :]
        prev = u_ref[pl.ds(jnp.maximum(c0 - 8, 0), 8), :]
        prev = jnp.where(c > 0, prev, 0.0)
        nxt = u_ref[pl.ds(jnp.minimum(c0 + chunk, t_len - 8), 8), :]
        nxt = jnp.where(c < n_chunks - 1, nxt, 0.0)
        x = jnp.concatenate([prev, main, nxt], axis=0)
        uc = (x[6:6 + chunk] * cw[0:1] + x[7:7 + chunk] * cw[1:2]
              + x[8:8 + chunk] * cw[2:3] + x[9:9 + chunk] * cw[3:4]) + cb
        uc_ref[pl.ds(c0, chunk), :] = uc
        a, b = coeffs(uc, 0)
        for k in range(n_blk):
            at, bt = tile_scan(a[8 * k:8 * k + 8], b[8 * k:8 * k + 8], False)
            ht = at * carry + bt
            h_ref[pl.ds(c0 + 8 * k, 8), :] = ht
            carry = jnp.broadcast_to(ht[7:8, :], (8, HEAD_DIM))
        return carry

    lax.fori_loop(0, n_chunks, fwd_chunk, jnp.zeros((8, HEAD_DIM), F32))

    def rev_chunk(ci, carry):
        c = n_chunks - 1 - ci
        c0 = pl.multiple_of(c * chunk, 8)
        uc = uc_ref[pl.ds(c0, chunk), :]
        a, b = coeffs(uc, 1)
        for k in reversed(range(n_blk)):
            at, bt = tile_scan(a[8 * k:8 * k + 8], b[8 * k:8 * k + 8], True)
            ht = at * carry + bt
            rows = pl.ds(c0 + 8 * k, 8)
            h_ref[rows, :] = h_ref[rows, :] + ht
            carry = jnp.broadcast_to(ht[0:1, :], (8, HEAD_DIM))
        rows = pl.ds(c0, chunk)
        o_ref[rows, :] = (h_ref[rows, :] * jax.nn.gelu(g_ref[rows, :])).astype(o_ref.dtype)
        return carry

    lax.fori_loop(0, n_chunks, rev_chunk, jnp.zeros((8, HEAD_DIM), F32))


def rglru(ug, conv_w, conv_b, w_a, b_a, w_x, b_x, lam, chunk=LRU_L):
    t_len = ug.shape[0]
    width = ug.shape[1] // 2
    nb = width // HEAD_DIM
    c = HEAD_DIM
    blk = (2 * _nbytes((t_len, c), F32) + _nbytes((t_len, c), BF16)
           + 4 * _nbytes((c, c), BF16))
    scratch = 2 * _nbytes((t_len, c), F32)
    return pl.pallas_call(
        functools.partial(_lru_kernel, chunk=chunk),
        grid=(nb,),
        in_specs=[pl.BlockSpec((t_len, c), lambda n: (0, n)),
                  pl.BlockSpec((t_len, c), lambda n: (0, n + nb)),
                  pl.BlockSpec((4, c), lambda n: (0, n)),
                  pl.BlockSpec((1, c), lambda n: (0, n)),
                  pl.BlockSpec((2, None, c, c), lambda n: (0, n, 0, 0)),
                  pl.BlockSpec((2, c), lambda n: (0, n)),
                  pl.BlockSpec((2, None, c, c), lambda n: (0, n, 0, 0)),
                  pl.BlockSpec((2, c), lambda n: (0, n)),
                  pl.BlockSpec((2, c), lambda n: (0, n))],
        out_specs=pl.BlockSpec((t_len, c), lambda n: (0, n)),
        out_shape=jax.ShapeDtypeStruct((t_len, width), BF16),
        scratch_shapes=[pltpu.VMEM((t_len, c), F32), pltpu.VMEM((t_len, c), F32)],
        compiler_params=_params(blk, scratch + 16 * _nbytes((chunk, c), F32)),
        name="rglru",
    )(ug, ug, conv_w, conv_b.reshape(1, width), w_a.astype(BF16), b_a, w_x.astype(BF16), b_x, lam)


_NT = (((1,), (1,)), ((), ()))


def _sink_softmax_pv(s, sink, v):
    d = HEAD_DIM
    fold = s[:, 0:LANES]
    for c in range(1, s.shape[1] // LANES):
        fold = jnp.maximum(fold, s[:, c * LANES:(c + 1) * LANES])
    m = jnp.maximum(jnp.max(fold, axis=1, keepdims=True), sink)
    p = jnp.exp(s - m).astype(BF16)
    pv = _dot(p, jnp.concatenate([v, jnp.ones_like(v)], axis=1))
    return pv[:, :d] / (pv[:, d:] + jnp.exp(sink - m))


def _swa_kernel(sink_ref, slope_ref, q_ref, k_ref, v_ref, o_ref):
    h = pl.program_id(0)
    n_real = q_ref.shape[0] - N_META
    n_blocks = n_real // BLOCK
    d = HEAD_DIM
    zpad = jnp.zeros((BLOCK - N_META, d), BF16)
    k_meta = jnp.concatenate([k_ref[0:N_META, :], zpad], axis=0)
    v_meta = jnp.concatenate([v_ref[0:N_META, :], zpad], axis=0)
    kw = 3 * BLOCK

    def per_head_rows(ref, rows):
        return jnp.concatenate([jnp.full((rows, 1), ref[GQA_GROUP * h + g], F32)
                                for g in range(GQA_GROUP)], axis=0)

    def stack_heads(x):
        return jnp.concatenate([x[:, g * d:(g + 1) * d] for g in range(GQA_GROUP)], axis=0)

    sink = per_head_rows(sink_ref, BLOCK)
    neg_slope = -per_head_rows(slope_ref, BLOCK)
    col = lax.broadcasted_iota(jnp.int32, (GQA_GROUP * BLOCK, BLOCK + kw), 1)
    rowq = lax.broadcasted_iota(jnp.int32, (GQA_GROUP * BLOCK, BLOCK + kw), 0) % BLOCK
    is_meta = col < N_META
    in_cols = col >= BLOCK
    col_off = col - BLOCK - rowq

    def bias_for(first_key_minus_first_query):
        rel = jnp.abs(col_off + first_key_minus_first_query)
        in_win = in_cols & (rel <= WINDOW)
        return jnp.where(is_meta, 0.0, jnp.where(in_win, neg_slope * rel.astype(F32), NEG_INF))

    def attend(j, ws, bias):
        r0, w0 = N_META + BLOCK * j, N_META + ws
        if not isinstance(j, int):
            r0, w0 = pl.multiple_of(r0, N_META), pl.multiple_of(w0, N_META)
        kcat = jnp.concatenate([k_meta, k_ref[pl.ds(w0, kw), :]], axis=0)
        vcat = jnp.concatenate([v_meta, v_ref[pl.ds(w0, kw), :]], axis=0)
        qs = stack_heads(q_ref[pl.ds(r0, BLOCK), :])
        s = lax.dot_general(qs, kcat, _NT, preferred_element_type=F32) * ATTN_SCALE + bias
        o = _sink_softmax_pv(s, sink, vcat)
        for g in range(GQA_GROUP):
            o_ref[pl.ds(r0, BLOCK), g * d:(g + 1) * d] = o[g * BLOCK:(g + 1) * BLOCK].astype(o_ref.dtype)

    attend(0, 0, bias_for(0))
    bias_mid = bias_for(-BLOCK)

    def block(j, carry):
        attend(j, BLOCK * (j - 1), bias_mid)
        return carry

    lax.fori_loop(1, n_blocks - 1, block, 0, unroll=4)
    attend(n_blocks - 1, n_real - kw, bias_for(n_real - kw - BLOCK * (n_blocks - 1)))

    kcat = jnp.concatenate([k_meta, k_ref[N_META:N_META + BLOCK, :]], axis=0)
    vcat = jnp.concatenate([v_meta, v_ref[N_META:N_META + BLOCK, :]], axis=0)
    colm = lax.broadcasted_iota(jnp.int32, (GQA_GROUP * N_META, 2 * BLOCK), 1)
    rowm = lax.broadcasted_iota(jnp.int32, (GQA_GROUP * N_META, 2 * BLOCK), 0) % N_META
    ok = (colm < N_META) | ((colm >= BLOCK) & (colm - BLOCK + N_META - rowm <= WINDOW))
    bias = jnp.where(ok, 0.0, NEG_INF)
    qs = stack_heads(q_ref[0:N_META, :])
    s = lax.dot_general(qs, kcat, _NT, preferred_element_type=F32) * ATTN_SCALE + bias
    o = _sink_softmax_pv(s, per_head_rows(sink_ref, N_META), vcat)
    for g in range(GQA_GROUP):
        o_ref[0:N_META, g * d:(g + 1) * d] = o[g * N_META:(g + 1) * N_META].astype(o_ref.dtype)


def windowed_attention(qkv, sink):
    t_len = qkv.shape[0]
    d = HEAD_DIM
    gw = GQA_GROUP * d
    slopes = jnp.asarray(2.0 ** (-8.0 * np.arange(1, SWA_Q_HEADS + 1) / SWA_Q_HEADS), F32)
    k_blk0 = SWA_Q_DIM // d
    v_blk0 = (SWA_Q_DIM + SWA_KV_DIM) // d
    blk = 2 * _nbytes((t_len, gw), BF16) + 2 * _nbytes((t_len, d), BF16)
    smem = pl.BlockSpec(memory_space=pltpu.SMEM)
    return pl.pallas_call(
        _swa_kernel,
        grid=(SWA_KV_HEADS,),
        in_specs=[smem, smem,
                  pl.BlockSpec((t_len, gw), lambda h: (0, h)),
                  pl.BlockSpec((t_len, d), lambda h: (0, k_blk0 + h)),
                  pl.BlockSpec((t_len, d), lambda h: (0, v_blk0 + h))],
        out_specs=pl.BlockSpec((t_len, gw), lambda h: (0, h)),
        out_shape=jax.ShapeDtypeStruct((t_len, SWA_Q_DIM), BF16),
        compiler_params=_params(blk, 8 * 2**20),
        name="windowed_attention",
    )(sink.astype(F32), slopes, qkv, qkv, qkv)


def _rope_tables(t_len, rows_pad):
    half = HEAD_DIM // 2
    n = t_len - N_META
    rows = n // GRID_W
    row = jnp.concatenate([jnp.zeros((N_META,), jnp.int32),
                           jnp.repeat(jnp.arange(rows, dtype=jnp.int32), GRID_W)])
    col = jnp.concatenate([jnp.zeros((N_META,), jnp.int32),
                           jnp.tile(jnp.arange(GRID_W, dtype=jnp.int32), rows)])
    inv_freq = ROPE_BASE ** (-(jnp.arange(half // 2, dtype=F32) * 2.0 / half))
    ang_row = row.astype(F32)[:, None] * inv_freq
    ang_col = col.astype(F32)[:, None] * inv_freq
    cos = jnp.concatenate([jnp.cos(ang_row), jnp.cos(ang_row), jnp.cos(ang_col), jnp.cos(ang_col)], axis=1)
    sin = jnp.concatenate([-jnp.sin(ang_row), jnp.sin(ang_row), -jnp.sin(ang_col), jnp.sin(ang_col)], axis=1)
    pad = ((0, rows_pad - t_len), (0, 0))
    return jnp.pad(cos, pad), jnp.pad(sin, pad)


def _qkv_prep_kernel(x_ref, cos_ref, sin_ref, qn_ref, kn_ref, q_ref, k_ref, v_ref, *, t_len):
    tm = x_ref.shape[0]
    d = HEAD_DIM
    cos = cos_ref[...]
    sin = sin_ref[...]
    lane = lax.broadcasted_iota(jnp.int32, (tm, d), 1)
    first = (lane % (d // 2)) < (d // 4)
    rows = pl.program_id(0) * tm + lax.broadcasted_iota(jnp.int32, (tm, 1), 0)
    valid = rows < t_len

    ones_rhs = jnp.ones((3 * d, d), BF16)

    def lane_sum(v):
        hi = v.astype(BF16)
        r1 = v - hi.astype(F32)
        mid = r1.astype(BF16)
        lo = (r1 - mid.astype(F32)).astype(BF16)
        return _dot(jnp.concatenate([hi, mid, lo], axis=1), ones_rhs)

    def norm_rope(xh, g):
        y = xh * lax.rsqrt(lane_sum(xh * xh) * (1.0 / d) + RMS_EPS) * g
        partner = jnp.where(first, pltpu.roll(y, d - d // 4, 1), pltpu.roll(y, d // 4, 1))
        return y * cos + partner * sin

    qn = qn_ref[...]
    kn = kn_ref[...]
    for hd in range(GA_Q_HEADS):
        q_ref[:, hd * d:(hd + 1) * d] = norm_rope(x_ref[:, hd * d:(hd + 1) * d], qn).astype(q_ref.dtype)
    k_aux = jnp.where(lane == 0, 1.0, jnp.where((lane == 1) & jnp.logical_not(valid), 1.0, 0.0))
    ones = jnp.ones((tm, d), v_ref.dtype)
    for hd in range(GA_KV_HEADS):
        kk = norm_rope(x_ref[:, GA_Q_DIM + hd * d:GA_Q_DIM + (hd + 1) * d], kn)
        k_ref[:, hd * EXT:hd * EXT + d] = jnp.where(valid, kk, 0.0).astype(k_ref.dtype)
        k_ref[:, hd * EXT + d:(hd + 1) * EXT] = k_aux.astype(k_ref.dtype)
        v = x_ref[:, GA_Q_DIM + GA_KV_DIM + hd * d:GA_Q_DIM + GA_KV_DIM + (hd + 1) * d]
        v_ref[:, hd * EXT:hd * EXT + d] = jnp.where(valid, v, 0.0).astype(v_ref.dtype)
        v_ref[:, hd * EXT + d:(hd + 1) * EXT] = ones


def qkv_prep(qkv, q_norm, k_norm, cos, sin, keys_pad=KEYS_PAD, tm=PREP_TM):
    t_len, width = qkv.shape
    d = HEAD_DIM
    blk = (_nbytes((tm, width), F32) + 2 * _nbytes((tm, d), F32)
           + _nbytes((tm, GA_Q_DIM), BF16) + 2 * _nbytes((tm, GA_KV_HEADS * EXT), BF16))
    return pl.pallas_call(
        functools.partial(_qkv_prep_kernel, t_len=t_len),
        grid=(keys_pad // tm,),
        in_specs=[pl.BlockSpec((tm, width), lambda i: (i, 0)),
                  pl.BlockSpec((tm, d), lambda i: (i, 0)),
                  pl.BlockSpec((tm, d), lambda i: (i, 0)),
                  pl.BlockSpec((1, d), lambda i: (0, 0)),
                  pl.BlockSpec((1, d), lambda i: (0, 0))],
        out_specs=[pl.BlockSpec((tm, GA_Q_DIM), lambda i: (i, 0)),
                   pl.BlockSpec((tm, GA_KV_HEADS * EXT), lambda i: (i, 0)),
                   pl.BlockSpec((tm, GA_KV_HEADS * EXT), lambda i: (i, 0))],
        out_shape=[jax.ShapeDtypeStruct((t_len, GA_Q_DIM), BF16),
                   jax.ShapeDtypeStruct((keys_pad, GA_KV_HEADS * EXT), BF16),
                   jax.ShapeDtypeStruct((keys_pad, GA_KV_HEADS * EXT), BF16)],
        compiler_params=_params(blk, 4 * 2**20),
        name="qkv_prep",
    )(qkv, cos, sin, q_norm.reshape(1, d), k_norm.reshape(1, d))


def _stacked_q_ext(q_ref, shift_ref):
    d = HEAD_DIM
    q = jnp.concatenate([q_ref[:, g * d:(g + 1) * d] for g in range(q_ref.shape[1] // d)], axis=0)
    lane = lax.broadcasted_iota(jnp.int32, q.shape, 1)
    aux = jnp.where(lane == 0, -shift_ref[0], jnp.where(lane == 1, -PAD_KEY_PENALTY, 0.0))
    return jnp.concatenate([q, aux.astype(q.dtype)], axis=1)


def _store_heads(o_ref, acc):
    d = HEAD_DIM
    tq = o_ref.shape[0]
    o = acc[:, :d] / acc[:, d:]
    for g in range(o_ref.shape[1] // d):
        o_ref[:, g * d:(g + 1) * d] = o[g * tq:(g + 1) * tq].astype(o_ref.dtype)


def _dense_fast_kernel(shift_ref, q_ref, k_ref, v_ref, o_ref, *, tk):
    q = _stacked_q_ext(q_ref, shift_ref)
    acc = None
    for c in range(k_ref.shape[0] // tk):
        s = lax.dot_general(q, k_ref[c * tk:(c + 1) * tk, :], _NT, preferred_element_type=F32)
        p = jnp.exp2(s * EXP2_SCALE).astype(BF16)
        pv = _dot(p, v_ref[c * tk:(c + 1) * tk, :])
        acc = pv if acc is None else acc + pv
    _store_heads(o_ref, acc)


def _dense_safe_kernel(shift_ref, q_ref, k_ref, v_ref, o_ref, *, tk, n_keys):
    q = _stacked_q_ext(q_ref, shift_ref)
    rows = q.shape[0]

    def chunk(c, carry):
        m, acc = carry
        k0 = pl.multiple_of(c * tk, tk)
        t = lax.dot_general(q, k_ref[pl.ds(k0, tk), :], _NT, preferred_element_type=F32) * EXP2_SCALE
        key = k0 + lax.broadcasted_iota(jnp.int32, (rows, tk), 1)
        t = jnp.where(key < n_keys, t, NEG_INF)
        m_new = jnp.maximum(m, jnp.max(t, axis=1, keepdims=True))
        p = jnp.exp2(t - m_new).astype(BF16)
        acc = jnp.exp2(m - m_new) * acc + _dot(p, v_ref[pl.ds(k0, tk), :])
        return m_new, acc

    init = (jnp.full((rows, 1), NEG_INF, F32), jnp.zeros((rows, EXT), F32))
    _, acc = lax.fori_loop(0, k_ref.shape[0] // tk, chunk, init)
    _store_heads(o_ref, acc)


def dense_attention(q, k_ext, v_ext, shift, fast, tq=DENSE_TQ, tk=DENSE_TK, heads=DENSE_HEADS):
    t_len = q.shape[0]
    d = HEAD_DIM
    kp = k_ext.shape[0]
    assert GQA_GROUP % heads == 0
    groups_per_kv = GQA_GROUP // heads
    blk = 2 * _nbytes((tq, heads * d), BF16) + 2 * _nbytes((kp, EXT), BF16)
    if fast:
        body = functools.partial(_dense_fast_kernel, tk=tk)
    else:
        body = functools.partial(_dense_safe_kernel, tk=tk, n_keys=t_len)
    return pl.pallas_call(
        body,
        grid=(GA_Q_HEADS // heads, t_len // tq),
        in_specs=[pl.BlockSpec(memory_space=pltpu.SMEM),
                  pl.BlockSpec((tq, heads * d), lambda h, i: (i, h)),
                  pl.BlockSpec((kp, EXT), lambda h, i: (0, h // groups_per_kv)),
                  pl.BlockSpec((kp, EXT), lambda h, i: (0, h // groups_per_kv))],
        out_specs=pl.BlockSpec((tq, heads * d), lambda h, i: (i, h)),
        out_shape=jax.ShapeDtypeStruct((t_len, GA_Q_DIM), BF16),
        compiler_params=_params(blk, 10 * _nbytes((heads * tq, tk), F32)),
        name="dense_attention_fast" if fast else "dense_attention_safe",
    )(shift, q, k_ext, v_ext)


def _mixer_ab(xg, rstd, i, w_in, conv_w, conv_b, w_a, b_a, w_x, b_x, lam, sink):
    ug, _ = matmul(xg, rstd, w_in, 0, 2 * LRU_WIDTH, F32, tn=1024)
    qkv, _ = matmul(xg, rstd, w_in, 2 * LRU_WIDTH, SWA_Q_DIM + 2 * SWA_KV_DIM, BF16, tn=1024)
    y_a = rglru(ug, conv_w[i], conv_b[i], w_a[i], b_a[i], w_x[i], b_x[i], lam[i])
    y_b = windowed_attention(qkv, sink[i])
    return [y_a, y_b]


def _mixer_c(xg, rstd, i, w_in, q_norm, k_norm, cos, sin):
    qkv, _ = matmul(xg, rstd, w_in, 0, GA_Q_DIM + 2 * GA_KV_DIM, F32, tn=1024)
    q, k_ext, v_ext = qkv_prep(qkv, q_norm[i], k_norm[i], cos, sin)
    shift = (HEAD_DIM * jnp.max(jnp.abs(q_norm[i])) * jnp.max(jnp.abs(k_norm[i]))).reshape(1)
    o = lax.cond(ATTN_SCALE * shift[0] <= FAST_SOFTMAX_MAX_LOGIT,
                 lambda a: dense_attention(*a, fast=True),
                 lambda a: dense_attention(*a, fast=False), (q, k_ext, v_ext, shift))
    return [o]


def kernel(x, meta_tokens, ffn1_norm, ffn1_w_gu, ffn1_w_down, mix_norm, ab_w_in, ab_w_out, lru_conv_w, lru_conv_b, lru_w_a, lru_b_a, lru_w_x, lru_b_x, lru_lambda, swa_sink, c_w_in, c_w_out, c_q_norm, c_k_norm, ffn2_norm, ffn2_w_gu, ffn2_w_down, final_norm):
    bsz, n, d = x.shape
    assert (bsz, n, d) == (1, SEQ, D_MODEL)
    cos, sin = _rope_tables(T_TOK, KEYS_PAD)
    h, xg, rstd = embed_stats(x[0], meta_tokens.astype(x.dtype), ffn1_norm[0])
    w_gu1 = ffn1_w_gu[0].astype(BF16)
    for layer in range(DEPTH):
        i = layer // 2
        if layer % 2 == 0:
            in_job, out_job = (ab_w_in, i, 128, AB_IN // 2), (ab_w_out, i, 64, D_MODEL)
        else:
            in_job, out_job = (c_w_in, i, 128, C_IN // 3), (c_w_out, i, 64, D_MODEL)
        act, (w_dn1, w_in) = matmul_swiglu(xg, rstd, w_gu1,
                                           cast_jobs=[(ffn1_w_down, layer, 64, D_MODEL), in_job])
        (h, xg, rstd), (w_out,) = matmul_residual([act], w_dn1, h, 0.5, g_next=mix_norm[layer],
                                                  cast_jobs=[out_job])
        if layer % 2 == 0:
            ys = _mixer_ab(xg, rstd, i, w_in, lru_conv_w, lru_conv_b, lru_w_a, lru_b_a,
                           lru_w_x, lru_b_x, lru_lambda, swa_sink)
        else:
            ys = _mixer_c(xg, rstd, i, w_in, c_q_norm, c_k_norm, cos, sin)
        (h, xg, rstd), (w_gu2,) = matmul_residual(ys, w_out, h, 1.0, g_next=ffn2_norm[layer],
                                                  cast_jobs=[(ffn2_w_gu, layer, 64, 2 * D_FF)])
        last = layer + 1 == DEPTH
        jobs = [(ffn2_w_down, layer, 64, D_MODEL)]
        if not last:
            jobs.append((ffn1_w_gu, layer + 1, 128, D_MODEL))
        act, cast = matmul_swiglu(xg, rstd, w_gu2, cast_jobs=jobs)
        if last:
            h, _ = matmul_residual([act], cast[0], h, 0.5)
        else:
            w_gu1 = cast[1]
            (h, xg, rstd), _ = matmul_residual([act], cast[0], h, 0.5, g_next=ffn1_norm[layer + 1])
    return final_rmsnorm(h, final_norm)[None]
```

```python
import functools

import numpy as np
import jax
import jax.numpy as jnp
from jax import lax
from jax.experimental import pallas as pl
from jax.experimental.pallas import tpu as pltpu

F32 = jnp.float32
BF16 = jnp.bfloat16

D_MODEL = 4096
SEQ = 8192
DEPTH = 4
HEAD_DIM = 128
N_META = 16
T_TOK = SEQ + N_META
GRID_W = 64
BLOCK = 128
WINDOW = 128
RMS_EPS = 1e-6
NEG_INF = -1e30
D_FF = (3 * D_MODEL) // 2
LRU_WIDTH = D_MODEL // 2
LRU_C = 8.0
SWA_Q_HEADS = 16
SWA_KV_HEADS = 4
SWA_Q_DIM = SWA_Q_HEADS * HEAD_DIM
SWA_KV_DIM = SWA_KV_HEADS * HEAD_DIM
GA_Q_HEADS = 32
GA_KV_HEADS = 8
GA_Q_DIM = GA_Q_HEADS * HEAD_DIM
GA_KV_DIM = GA_KV_HEADS * HEAD_DIM
GQA_GROUP = 4
AB_IN = 2 * LRU_WIDTH + SWA_Q_DIM + 2 * SWA_KV_DIM
C_IN = GA_Q_DIM + 2 * GA_KV_DIM
ROPE_BASE = 10000.0
ATTN_SCALE = HEAD_DIM ** -0.5

V7X_VMEM_BYTES = 64 * 2**20
VMEM_CAP_BYTES = V7X_VMEM_BYTES - 6 * 2**20

TM = 912
LANES = 128
MXU_DIM = 256
KEYS_PAD = 33 * MXU_DIM
DENSE_TK = MXU_DIM
DENSE_TQ = 912
DENSE_HEADS = 2
PREP_TM = 384
NORM_TM = 432
LRU_L = 432
EXT = 2 * HEAD_DIM
EXP2_SCALE = ATTN_SCALE * float(np.log2(np.e))
PAD_KEY_PENALTY = 30000.0
FAST_SOFTMAX_MAX_LOGIT = 40.0


def _nbytes(shape, dtype):
    return int(np.prod(shape)) * jnp.dtype(dtype).itemsize


def _params(block_bytes, extra_bytes=0):
    need = 2 * block_bytes + extra_bytes + 4 * 2**20
    return pltpu.CompilerParams(vmem_limit_bytes=int(min(max(need, 16 * 2**20), VMEM_CAP_BYTES)))


def _embed_stats_kernel(meta_ref, tail_ref, main_ref, g_ref, h_ref, xg_ref, rstd_ref):
    tm = h_ref.shape[0]
    head = jnp.where(pl.program_id(0) == 0, meta_ref[...], tail_ref[...])
    x = jnp.concatenate([head, main_ref[:tm - N_META, :]], axis=0)
    h_ref[...] = x
    xg_ref[...] = (x * g_ref[...]).astype(xg_ref.dtype)
    rstd = lax.rsqrt(jnp.mean(x * x, axis=-1, keepdims=True) + RMS_EPS)
    rstd_ref[...] = jnp.broadcast_to(rstd, rstd_ref.shape)


def embed_stats(x, meta, g, tm=NORM_TM):
    n, d = x.shape
    m = n + N_META
    assert tm % N_META == 0 and m % tm == 0
    per = tm // N_META
    blk = (2 * _nbytes((tm, d), F32) + 2 * _nbytes((N_META, d), F32) + _nbytes((tm, d), BF16)
           + _nbytes((tm, LANES), F32))
    return pl.pallas_call(
        _embed_stats_kernel,
        grid=(m // tm,),
        in_specs=[pl.BlockSpec((N_META, d), lambda i: (0, 0)),
                  pl.BlockSpec((N_META, d), lambda i: (jnp.maximum(i * per - 1, 0), 0)),
                  pl.BlockSpec((tm, d), lambda i: (i, 0)),
                  pl.BlockSpec((1, d), lambda i: (0, 0))],
        out_specs=[pl.BlockSpec((tm, d), lambda i: (i, 0)),
                   pl.BlockSpec((tm, d), lambda i: (i, 0)),
                   pl.BlockSpec((tm, LANES), lambda i: (i, 0))],
        out_shape=[jax.ShapeDtypeStruct((m, d), F32),
                   jax.ShapeDtypeStruct((m, d), BF16),
                   jax.ShapeDtypeStruct((m, LANES), F32)],
        compiler_params=_params(blk, _nbytes((tm, d), F32)),
        name="embed_stats",
    )(meta, x, x, g.reshape(1, d))


def _final_norm_kernel(a_ref, b_ref, g_ref, o_ref):
    x = jnp.concatenate([a_ref[N_META:, :], b_ref[...]], axis=0)
    y = x * lax.rsqrt(jnp.mean(x * x, axis=-1, keepdims=True) + RMS_EPS)
    o_ref[...] = y * g_ref[...]


def final_rmsnorm(h, g, tm=256):
    m, d = h.shape
    n = m - N_META
    blk = 2 * _nbytes((tm, d), F32) + _nbytes((N_META, d), F32)
    return pl.pallas_call(
        _final_norm_kernel,
        grid=(n // tm,),
        in_specs=[pl.BlockSpec((tm, d), lambda i: (i, 0)),
                  pl.BlockSpec((N_META, d), lambda i: ((i + 1) * (tm // N_META), 0)),
                  pl.BlockSpec((1, d), lambda i: (0, 0))],
        out_specs=pl.BlockSpec((tm, d), lambda i: (i, 0)),
        out_shape=jax.ShapeDtypeStruct((n, d), F32),
        compiler_params=_params(blk, 2 * _nbytes((tm, d), F32)),
        name="final_norm",
    )(h, h, g.reshape(1, d))


def _dot(a, b):
    return jnp.dot(a, b, preferred_element_type=F32)


def _lane_tile(v, width):
    return jnp.concatenate([v] * (width // LANES), axis=1)


def _cast_specs(jobs, n_col_tiles, steps):
    in_specs, out_specs, out_shape, args, nbytes = [], [], [], [], 0
    for w, layer, rb, cb in jobs:
        _, k, n = w.shape
        assert k % rb == 0 and n % cb == 0 and (k // rb) * (n // cb) <= steps
        nc = n // cb
        last = (k // rb) * nc - 1

        def slab(i, j, nc=nc, last=last):
            t = jnp.minimum(i * n_col_tiles + j, last)
            return t // nc, t % nc

        in_specs.append(pl.BlockSpec((None, rb, cb),
                                     functools.partial(lambda i, j, f, l: (l, *f(i, j)), f=slab, l=layer)))
        out_specs.append(pl.BlockSpec((rb, cb), slab))
        out_shape.append(jax.ShapeDtypeStruct((k, n), BF16))
        args.append(w)
        nbytes += _nbytes((rb, cb), F32) + _nbytes((rb, cb), BF16)
    return in_specs, out_specs, out_shape, args, nbytes


def _run_cast_jobs(src_refs, dst_refs):
    for src, dst in zip(src_refs, dst_refs):
        dst[...] = src[...].astype(dst.dtype)


def _mm_kernel(*refs, n_cast):
    x_ref, rstd_ref, w_ref = refs[:3]
    o_ref = refs[3 + n_cast]
    _run_cast_jobs(refs[3:3 + n_cast], refs[4 + n_cast:])
    acc = _dot(x_ref[...], w_ref[...])
    o_ref[...] = (acc * _lane_tile(rstd_ref[...], acc.shape[1])).astype(o_ref.dtype)


def matmul(xg, rstd, w, n_off, n, out_dtype, tn, cast_jobs=(), tm=TM):
    m, k = xg.shape
    joff = n_off // tn
    grid = (m // tm, n // tn)
    c_in, c_out, c_shape, c_args, c_bytes = _cast_specs(cast_jobs, grid[1], grid[0] * grid[1])
    blk = (_nbytes((tm, k), BF16) + _nbytes((tm, LANES), F32) + _nbytes((k, tn), BF16)
           + _nbytes((tm, tn), out_dtype) + c_bytes)
    out = pl.pallas_call(
        functools.partial(_mm_kernel, n_cast=len(cast_jobs)),
        grid=grid,
        in_specs=[pl.BlockSpec((tm, k), lambda i, j: (i, 0)),
                  pl.BlockSpec((tm, LANES), lambda i, j: (i, 0)),
                  pl.BlockSpec((k, tn), lambda i, j: (0, j + joff)), *c_in],
        out_specs=[pl.BlockSpec((tm, tn), lambda i, j: (i, j)), *c_out],
        out_shape=[jax.ShapeDtypeStruct((m, n), out_dtype), *c_shape],
        compiler_params=_params(blk, 2 * _nbytes((tm, tn), F32)),
        name="matmul",
    )(xg, rstd, w, *c_args)
    return out[0], out[1:]


def _mm_swiglu_kernel(*refs, n_cast):
    x_ref, rstd_ref, wg_ref, wu_ref = refs[:4]
    o_ref = refs[4 + n_cast]
    _run_cast_jobs(refs[4:4 + n_cast], refs[5 + n_cast:])
    x = x_ref[...]
    rstd = _lane_tile(rstd_ref[...], o_ref.shape[1])
    gate = _dot(x, wg_ref[...]) * rstd
    up = _dot(x, wu_ref[...]) * rstd
    o_ref[...] = (gate * (0.5 + 0.5 * jnp.tanh(0.5 * gate)) * up).astype(o_ref.dtype)


def matmul_swiglu(xg, rstd, w_gu, cast_jobs=(), tn=512, tm=TM):
    m, k = xg.shape
    f = w_gu.shape[1] // 2
    uoff = f // tn
    grid = (m // tm, f // tn)
    c_in, c_out, c_shape, c_args, c_bytes = _cast_specs(cast_jobs, grid[1], grid[0] * grid[1])
    blk = (_nbytes((tm, k), BF16) + _nbytes((tm, LANES), F32) + 2 * _nbytes((k, tn), BF16)
           + _nbytes((tm, tn), BF16) + c_bytes)
    out = pl.pallas_call(
        functools.partial(_mm_swiglu_kernel, n_cast=len(cast_jobs)),
        grid=grid,
        in_specs=[pl.BlockSpec((tm, k), lambda i, j: (i, 0)),
                  pl.BlockSpec((tm, LANES), lambda i, j: (i, 0)),
                  pl.BlockSpec((k, tn), lambda i, j: (0, j)),
                  pl.BlockSpec((k, tn), lambda i, j: (0, j + uoff)), *c_in],
        out_specs=[pl.BlockSpec((tm, tn), lambda i, j: (i, j)), *c_out],
        out_shape=[jax.ShapeDtypeStruct((m, f), BF16), *c_shape],
        compiler_params=_params(blk, 3 * _nbytes((tm, tn), F32)),
        name="matmul_swiglu",
    )(xg, rstd, w_gu, w_gu, *c_args)
    return out[0], out[1:]


def _mm_res_kernel(*refs, n_x, n_cast, scale, with_stats, d_model):
    x_refs, w_refs = refs[:n_x], refs[n_x:2 * n_x]
    r_ref = refs[2 * n_x]
    n_in = 2 * n_x + 1 + int(with_stats)
    n_out = 3 if with_stats else 1
    outs = refs[n_in + n_cast:n_in + n_cast + n_out]
    _run_cast_jobs(refs[n_in:n_in + n_cast], refs[n_in + n_cast + n_out:n_in + 2 * n_cast + n_out])
    acc = _dot(x_refs[0][...], w_refs[0][...])
    for x_ref, w_ref in zip(x_refs[1:], w_refs[1:]):
        acc = acc + _dot(x_ref[...], w_ref[...])
    h = r_ref[...] + scale * acc
    if not with_stats:
        outs[0][...] = h
        return
    g_ref, ssq_ref = refs[n_in - 1], refs[-1]
    o_ref, xg_ref, rstd_ref = outs
    o_ref[...] = h
    xg_ref[...] = (h * g_ref[...]).astype(xg_ref.dtype)
    sq = h * h
    part = sq[:, 0:LANES]
    for c in range(1, sq.shape[1] // LANES):
        part = part + sq[:, c * LANES:(c + 1) * LANES]
    j = pl.program_id(1)

    @pl.when(j == 0)
    def _():
        ssq_ref[...] = part

    @pl.when(j > 0)
    def _():
        ssq_ref[...] += part

    @pl.when(j == pl.num_programs(1) - 1)
    def _():
        tot = jnp.sum(ssq_ref[...], axis=-1, keepdims=True)
        rstd_ref[...] = jnp.broadcast_to(lax.rsqrt(tot * (1.0 / d_model) + RMS_EPS), rstd_ref.shape)


def matmul_residual(xs, w, res, scale, g_next=None, cast_jobs=(), tn=512, tm=TM):
    m, k = xs[0].shape
    n_x = len(xs)
    assert all(x.shape == (m, k) for x in xs)
    n = w.shape[1]
    with_stats = g_next is not None
    grid = (m // tm, n // tn)
    c_in, c_out, c_shape, c_args, c_bytes = _cast_specs(cast_jobs, grid[1], grid[0] * grid[1])
    blk = (n_x * (_nbytes((tm, k), BF16) + _nbytes((k, tn), BF16)) + 2 * _nbytes((tm, tn), F32)
           + _nbytes((tm, tn), BF16) + _nbytes((tm, LANES), F32) + c_bytes)
    in_specs = [pl.BlockSpec((tm, k), lambda i, j: (i, 0)) for _ in xs]
    in_specs += [pl.BlockSpec((k, tn), functools.partial(lambda i, j, c: (c, j), c=c))
                 for c in range(n_x)]
    in_specs.append(pl.BlockSpec((tm, tn), lambda i, j: (i, j)))
    out_specs = [pl.BlockSpec((tm, tn), lambda i, j: (i, j))]
    out_shape = [jax.ShapeDtypeStruct((m, n), F32)]
    args = [*xs, *([w] * n_x), res]
    scratch = []
    if with_stats:
        in_specs.append(pl.BlockSpec((1, tn), lambda i, j: (0, j)))
        out_specs += [pl.BlockSpec((tm, tn), lambda i, j: (i, j)),
                      pl.BlockSpec((tm, LANES), lambda i, j: (i, 0))]
        out_shape += [jax.ShapeDtypeStruct((m, n), BF16), jax.ShapeDtypeStruct((m, LANES), F32)]
        args.append(g_next.reshape(1, n))
        scratch = [pltpu.VMEM((tm, LANES), F32)]
    n_main = len(out_shape)
    out = pl.pallas_call(
        functools.partial(_mm_res_kernel, n_x=n_x, n_cast=len(cast_jobs), scale=scale,
                          with_stats=with_stats, d_model=n),
        grid=grid,
        in_specs=[*in_specs, *c_in],
        out_specs=[*out_specs, *c_out],
        out_shape=[*out_shape, *c_shape],
        scratch_shapes=scratch,
        compiler_params=_params(blk, 3 * _nbytes((tm, tn), F32)),
        name="matmul_residual",
    )(*args, *c_args)
    main = out[:n_main]
    return (main if with_stats else main[0]), out[n_main:]


def _softplus(z):
    return jnp.maximum(z, 0.0) + jnp.log1p(jnp.exp(-jnp.abs(z)))


def _lru_kernel(u_ref, g_ref, cw_ref, cb_ref, wa_ref, ba_ref, wx_ref, bx_ref, lam_ref,
                o_ref, uc_ref, h_ref, *, chunk):
    t_len = u_ref.shape[0]
    n_chunks = t_len // chunk
    n_blk = chunk // 8
    cw = cw_ref[...]
    cb = cb_ref[...]
    row8 = lax.broadcasted_iota(jnp.int32, (8, HEAD_DIM), 0)

    def coeffs(uc, d):
        ucb = uc.astype(BF16)
        gate_r = _dot(ucb, wa_ref[d]) + ba_ref[d:d + 1, :]
        gate_i = _dot(ucb, wx_ref[d]) + bx_ref[d:d + 1, :]
        r = jax.nn.sigmoid(gate_r)
        i = 0.5 + 0.5 * jnp.tanh(0.5 * gate_i)
        log_a = (-LRU_C * r) * _softplus(-lam_ref[d:d + 1, :])
        a = jnp.exp(log_a)
        th = jnp.tanh(log_a)
        n = -2.0 * th
        coef = jnp.where(n > 0.0, n * lax.rsqrt(n * (1.0 - th)), 0.0)
        b = coef * (i * uc)
        return a, b

    def tile_scan(a, b, reverse):
        for s in (1, 2, 4):
            if reverse:
                keep = row8 < 8 - s
                shift = 8 - s
            else:
                keep = row8 >= s
                shift = s
            a_sh = jnp.where(keep, pltpu.roll(a, shift, 0), 1.0)
            b_sh = jnp.where(keep, pltpu.roll(b, shift, 0), 0.0)
            b = a * b_sh + b
            a = a * a_sh
        return a, b

    def fwd_chunk(c, carry):
        c0 = pl.multiple_of(c * chunk, 8)
        main = u_ref[pl.ds(c0, chunk), :]
        prev = u_ref[pl.ds(jnp.maximum(c0 - 8, 0), 8), :]
        prev = jnp.where(c > 0, prev, 0.0)
        nxt = u_ref[pl.ds(jnp.minimum(c0 + chunk, t_len - 8), 8), :]
        nxt = jnp.where(c < n_chunks - 1, nxt, 0.0)
        x = jnp.concatenate([prev, main, nxt], axis=0)
        uc = (x[6:6 + chunk] * cw[0:1] + x[7:7 + chunk] * cw[1:2]
              + x[8:8 + chunk] * cw[2:3] + x[9:9 + chunk] * cw[3:4]) + cb
        uc_ref[pl.ds(c0, chunk), :] = uc
        a, b = coeffs(uc, 0)
        for k in range(n_blk):
            at, bt = tile_scan(a[8 * k:8 * k + 8], b[8 * k:8 * k + 8], False)
            ht = at * carry + bt
            h_ref[pl.ds(c0 + 8 * k, 8), :] = ht
            carry = jnp.broadcast_to(ht[7:8, :], (8, HEAD_DIM))
        return carry

    lax.fori_loop(0, n_chunks, fwd_chunk, jnp.zeros((8, HEAD_DIM), F32))

    def rev_chunk(ci, carry):
        c = n_chunks - 1 - ci
        c0 = pl.multiple_of(c * chunk, 8)
        uc = uc_ref[pl.ds(c0, chunk), :]
        a, b = coeffs(uc, 1)
        for k in reversed(range(n_blk)):
            at, bt = tile_scan(a[8 * k:8 * k + 8], b[8 * k:8 * k + 8], True)
            ht = at * carry + bt
            rows = pl.ds(c0 + 8 * k, 8)
            h_ref[rows, :] = h_ref[rows, :] + ht
            carry = jnp.broadcast_to(ht[0:1, :], (8, HEAD_DIM))
        rows = pl.ds(c0, chunk)
        o_ref[rows, :] = (h_ref[rows, :] * jax.nn.gelu(g_ref[rows, :])).astype(o_ref.dtype)
        return carry

    lax.fori_loop(0, n_chunks, rev_chunk, jnp.zeros((8, HEAD_DIM), F32))


def rglru(ug, conv_w, conv_b, w_a, b_a, w_x, b_x, lam, chunk=LRU_L):
    t_len = ug.shape[0]
    width = ug.shape[1] // 2
    nb = width // HEAD_DIM
    c = HEAD_DIM
    blk = (2 * _nbytes((t_len, c), F32) + _nbytes((t_len, c), BF16)
           + 4 * _nbytes((c, c), BF16))
    scratch = 2 * _nbytes((t_len, c), F32)
    return pl.pallas_call(
        functools.partial(_lru_kernel, chunk=chunk),
        grid=(nb,),
        in_specs=[pl.BlockSpec((t_len, c), lambda n: (0, n)),
                  pl.BlockSpec((t_len, c), lambda n: (0, n + nb)),
                  pl.BlockSpec((4, c), lambda n: (0, n)),
                  pl.BlockSpec((1, c), lambda n: (0, n)),
                  pl.BlockSpec((2, None, c, c), lambda n: (0, n, 0, 0)),
                  pl.BlockSpec((2, c), lambda n: (0, n)),
                  pl.BlockSpec((2, None, c, c), lambda n: (0, n, 0, 0)),
                  pl.BlockSpec((2, c), lambda n: (0, n)),
                  pl.BlockSpec((2, c), lambda n: (0, n))],
        out_specs=pl.BlockSpec((t_len, c), lambda n: (0, n)),
        out_shape=jax.ShapeDtypeStruct((t_len, width), BF16),
        scratch_shapes=[pltpu.VMEM((t_len, c), F32), pltpu.VMEM((t_len, c), F32)],
        compiler_params=_params(blk, scratch + 16 * _nbytes((chunk, c), F32)),
        name="rglru",
    )(ug, ug, conv_w, conv_b.reshape(1, width), w_a.astype(BF16), b_a, w_x.astype(BF16), b_x, lam)


_NT = (((1,), (1,)), ((), ()))


def _sink_softmax_pv(s, sink, v):
    d = HEAD_DIM
    fold = s[:, 0:LANES]
    for c in range(1, s.shape[1] // LANES):
        fold = jnp.maximum(fold, s[:, c * LANES:(c + 1) * LANES])
    m = jnp.maximum(jnp.max(fold, axis=1, keepdims=True), sink)
    p = jnp.exp(s - m).astype(BF16)
    pv = _dot(p, jnp.concatenate([v, jnp.ones_like(v)], axis=1))
    return pv[:, :d] / (pv[:, d:] + jnp.exp(sink - m))


def _swa_kernel(sink_ref, slope_ref, q_ref, k_ref, v_ref, o_ref):
    h = pl.program_id(0)
    n_real = q_ref.shape[0] - N_META
    n_blocks = n_real // BLOCK
    d = HEAD_DIM
    zpad = jnp.zeros((BLOCK - N_META, d), BF16)
    k_meta = jnp.concatenate([k_ref[0:N_META, :], zpad], axis=0)
    v_meta = jnp.concatenate([v_ref[0:N_META, :], zpad], axis=0)
    kw = 3 * BLOCK

    def per_head_rows(ref, rows):
        return jnp.concatenate([jnp.full((rows, 1), ref[GQA_GROUP * h + g], F32)
                                for g in range(GQA_GROUP)], axis=0)

    def stack_heads(x):
        return jnp.concatenate([x[:, g * d:(g + 1) * d] for g in range(GQA_GROUP)], axis=0)

    sink = per_head_rows(sink_ref, BLOCK)
    neg_slope = -per_head_rows(slope_ref, BLOCK)
    col = lax.broadcasted_iota(jnp.int32, (GQA_GROUP * BLOCK, BLOCK + kw), 1)
    rowq = lax.broadcasted_iota(jnp.int32, (GQA_GROUP * BLOCK, BLOCK + kw), 0) % BLOCK
    is_meta = col < N_META
    in_cols = col >= BLOCK
    col_off = col - BLOCK - rowq

    def bias_for(first_key_minus_first_query):
        rel = jnp.abs(col_off + first_key_minus_first_query)
        in_win = in_cols & (rel <= WINDOW)
        return jnp.where(is_meta, 0.0, jnp.where(in_win, neg_slope * rel.astype(F32), NEG_INF))

    def attend(j, ws, bias):
        r0, w0 = N_META + BLOCK * j, N_META + ws
        if not isinstance(j, int):
            r0, w0 = pl.multiple_of(r0, N_META), pl.multiple_of(w0, N_META)
        kcat = jnp.concatenate([k_meta, k_ref[pl.ds(w0, kw), :]], axis=0)
        vcat = jnp.concatenate([v_meta, v_ref[pl.ds(w0, kw), :]], axis=0)
        qs = stack_heads(q_ref[pl.ds(r0, BLOCK), :])
        s = lax.dot_general(qs, kcat, _NT, preferred_element_type=F32) * ATTN_SCALE + bias
        o = _sink_softmax_pv(s, sink, vcat)
        for g in range(GQA_GROUP):
            o_ref[pl.ds(r0, BLOCK), g * d:(g + 1) * d] = o[g * BLOCK:(g + 1) * BLOCK].astype(o_ref.dtype)

    attend(0, 0, bias_for(0))
    bias_mid = bias_for(-BLOCK)

    def block(j, carry):
        attend(j, BLOCK * (j - 1), bias_mid)
        return carry

    lax.fori_loop(1, n_blocks - 1, block, 0, unroll=4)
    attend(n_blocks - 1, n_real - kw, bias_for(n_real - kw - BLOCK * (n_blocks - 1)))

    kcat = jnp.concatenate([k_meta, k_ref[N_META:N_META + BLOCK, :]], axis=0)
    vcat = jnp.concatenate([v_meta, v_ref[N_META:N_META + BLOCK, :]], axis=0)
    colm = lax.broadcasted_iota(jnp.int32, (GQA_GROUP * N_META, 2 * BLOCK), 1)
    rowm = lax.broadcasted_iota(jnp.int32, (GQA_GROUP * N_META, 2 * BLOCK), 0) % N_META
    ok = (colm < N_META) | ((colm >= BLOCK) & (colm - BLOCK + N_META - rowm <= WINDOW))
    bias = jnp.where(ok, 0.0, NEG_INF)
    qs = stack_heads(q_ref[0:N_META, :])
    s = lax.dot_general(qs, kcat, _NT, preferred_element_type=F32) * ATTN_SCALE + bias
    o = _sink_softmax_pv(s, per_head_rows(sink_ref, N_META), vcat)
    for g in range(GQA_GROUP):
        o_ref[0:N_META, g * d:(g + 1) * d] = o[g * N_META:(g + 1) * N_META].astype(o_ref.dtype)


def windowed_attention(qkv, sink):
    t_len = qkv.shape[0]
    d = HEAD_DIM
    gw = GQA_GROUP * d
    slopes = jnp.asarray(2.0 ** (-8.0 * np.arange(1, SWA_Q_HEADS + 1) / SWA_Q_HEADS), F32)
    k_blk0 = SWA_Q_DIM // d
    v_blk0 = (SWA_Q_DIM + SWA_KV_DIM) // d
    blk = 2 * _nbytes((t_len, gw), BF16) + 2 * _nbytes((t_len, d), BF16)
    smem = pl.BlockSpec(memory_space=pltpu.SMEM)
    return pl.pallas_call(
        _swa_kernel,
        grid=(SWA_KV_HEADS,),
        in_specs=[smem, smem,
                  pl.BlockSpec((t_len, gw), lambda h: (0, h)),
                  pl.BlockSpec((t_len, d), lambda h: (0, k_blk0 + h)),
                  pl.BlockSpec((t_len, d), lambda h: (0, v_blk0 + h))],
        out_specs=pl.BlockSpec((t_len, gw), lambda h: (0, h)),
        out_shape=jax.ShapeDtypeStruct((t_len, SWA_Q_DIM), BF16),
        compiler_params=_params(blk, 8 * 2**20),
        name="windowed_attention",
    )(sink.astype(F32), slopes, qkv, qkv, qkv)


def _rope_tables(t_len, rows_pad):
    half = HEAD_DIM // 2
    n = t_len - N_META
    rows = n // GRID_W
    row = jnp.concatenate([jnp.zeros((N_META,), jnp.int32),
                           jnp.repeat(jnp.arange(rows, dtype=jnp.int32), GRID_W)])
    col = jnp.concatenate([jnp.zeros((N_META,), jnp.int32),
                           jnp.tile(jnp.arange(GRID_W, dtype=jnp.int32), rows)])
    inv_freq = ROPE_BASE ** (-(jnp.arange(half // 2, dtype=F32) * 2.0 / half))
    ang_row = row.astype(F32)[:, None] * inv_freq
    ang_col = col.astype(F32)[:, None] * inv_freq
    cos = jnp.concatenate([jnp.cos(ang_row), jnp.cos(ang_row), jnp.cos(ang_col), jnp.cos(ang_col)], axis=1)
    sin = jnp.concatenate([-jnp.sin(ang_row), jnp.sin(ang_row), -jnp.sin(ang_col), jnp.sin(ang_col)], axis=1)
    pad = ((0, rows_pad - t_len), (0, 0))
    return jnp.pad(cos, pad), jnp.pad(sin, pad)


def _qkv_prep_kernel(x_ref, cos_ref, sin_ref, qn_ref, kn_ref, q_ref, k_ref, v_ref, *, t_len):
    tm = x_ref.shape[0]
    d = HEAD_DIM
    cos = cos_ref[...]
    sin = sin_ref[...]
    lane = lax.broadcasted_iota(jnp.int32, (tm, d), 1)
    first = (lane % (d // 2)) < (d // 4)
    rows = pl.program_id(0) * tm + lax.broadcasted_iota(jnp.int32, (tm, 1), 0)
    valid = rows < t_len

    ones_rhs = jnp.ones((3 * d, d), BF16)

    def lane_sum(v):
        hi = v.astype(BF16)
        r1 = v - hi.astype(F32)
        mid = r1.astype(BF16)
        lo = (r1 - mid.astype(F32)).astype(BF16)
        return _dot(jnp.concatenate([hi, mid, lo], axis=1), ones_rhs)

    def norm_rope(xh, g):
        y = xh * lax.rsqrt(lane_sum(xh * xh) * (1.0 / d) + RMS_EPS) * g
        partner = jnp.where(first, pltpu.roll(y, d - d // 4, 1), pltpu.roll(y, d // 4, 1))
        return y * cos + partner * sin

    qn = qn_ref[...]
    kn = kn_ref[...]
    for hd in range(GA_Q_HEADS):
        q_ref[:, hd * d:(hd + 1) * d] = norm_rope(x_ref[:, hd * d:(hd + 1) * d], qn).astype(q_ref.dtype)
    k_aux = jnp.where(lane == 0, 1.0, jnp.where((lane == 1) & jnp.logical_not(valid), 1.0, 0.0))
    ones = jnp.ones((tm, d), v_ref.dtype)
    for hd in range(GA_KV_HEADS):
        kk = norm_rope(x_ref[:, GA_Q_DIM + hd * d:GA_Q_DIM + (hd + 1) * d], kn)
        k_ref[:, hd * EXT:hd * EXT + d] = jnp.where(valid, kk, 0.0).astype(k_ref.dtype)
        k_ref[:, hd * EXT + d:(hd + 1) * EXT] = k_aux.astype(k_ref.dtype)
        v = x_ref[:, GA_Q_DIM + GA_KV_DIM + hd * d:GA_Q_DIM + GA_KV_DIM + (hd + 1) * d]
        v_ref[:, hd * EXT:hd * EXT + d] = jnp.where(valid, v, 0.0).astype(v_ref.dtype)
        v_ref[:, hd * EXT + d:(hd + 1) * EXT] = ones


def qkv_prep(qkv, q_norm, k_norm, cos, sin, keys_pad=KEYS_PAD, tm=PREP_TM):
    t_len, width = qkv.shape
    d = HEAD_DIM
    blk = (_nbytes((tm, width), F32) + 2 * _nbytes((tm, d), F32)
           + _nbytes((tm, GA_Q_DIM), BF16) + 2 * _nbytes((tm, GA_KV_HEADS * EXT), BF16))
    return pl.pallas_call(
        functools.partial(_qkv_prep_kernel, t_len=t_len),
        grid=(keys_pad // tm,),
        in_specs=[pl.BlockSpec((tm, width), lambda i: (i, 0)),
                  pl.BlockSpec((tm, d), lambda i: (i, 0)),
                  pl.BlockSpec((tm, d), lambda i: (i, 0)),
                  pl.BlockSpec((1, d), lambda i: (0, 0)),
                  pl.BlockSpec((1, d), lambda i: (0, 0))],
        out_specs=[pl.BlockSpec((tm, GA_Q_DIM), lambda i: (i, 0)),
                   pl.BlockSpec((tm, GA_KV_HEADS * EXT), lambda i: (i, 0)),
                   pl.BlockSpec((tm, GA_KV_HEADS * EXT), lambda i: (i, 0))],
        out_shape=[jax.ShapeDtypeStruct((t_len, GA_Q_DIM), BF16),
                   jax.ShapeDtypeStruct((keys_pad, GA_KV_HEADS * EXT), BF16),
                   jax.ShapeDtypeStruct((keys_pad, GA_KV_HEADS * EXT), BF16)],
        compiler_params=_params(blk, 4 * 2**20),
        name="qkv_prep",
    )(qkv, cos, sin, q_norm.reshape(1, d), k_norm.reshape(1, d))


def _stacked_q_ext(q_ref, shift_ref):
    d = HEAD_DIM
    q = jnp.concatenate([q_ref[:, g * d:(g + 1) * d] for g in range(q_ref.shape[1] // d)], axis=0)
    lane = lax.broadcasted_iota(jnp.int32, q.shape, 1)
    aux = jnp.where(lane == 0, -shift_ref[0], jnp.where(lane == 1, -PAD_KEY_PENALTY, 0.0))
    return jnp.concatenate([q, aux.astype(q.dtype)], axis=1)


def _store_heads(o_ref, acc):
    d = HEAD_DIM
    tq = o_ref.shape[0]
    o = acc[:, :d] / acc[:, d:]
    for g in range(o_ref.shape[1] // d):
        o_ref[:, g * d:(g + 1) * d] = o[g * tq:(g + 1) * tq].astype(o_ref.dtype)


def _dense_fast_kernel(shift_ref, q_ref, k_ref, v_ref, o_ref, *, tk):
    q = _stacked_q_ext(q_ref, shift_ref)
    acc = None
    for c in range(k_ref.shape[0] // tk):
        s = lax.dot_general(q, k_ref[c * tk:(c + 1) * tk, :], _NT, preferred_element_type=F32)
        p = jnp.exp2(s * EXP2_SCALE).astype(BF16)
        pv = _dot(p, v_ref[c * tk:(c + 1) * tk, :])
        acc = pv if acc is None else acc + pv
    _store_heads(o_ref, acc)


def _dense_safe_kernel(shift_ref, q_ref, k_ref, v_ref, o_ref, *, tk, n_keys):
    q = _stacked_q_ext(q_ref, shift_ref)
    rows = q.shape[0]

    def chunk(c, carry):
        m, acc = carry
        k0 = pl.multiple_of(c * tk, tk)
        t = lax.dot_general(q, k_ref[pl.ds(k0, tk), :], _NT, preferred_element_type=F32) * EXP2_SCALE
        key = k0 + lax.broadcasted_iota(jnp.int32, (rows, tk), 1)
        t = jnp.where(key < n_keys, t, NEG_INF)
        m_new = jnp.maximum(m, jnp.max(t, axis=1, keepdims=True))
        p = jnp.exp2(t - m_new).astype(BF16)
        acc = jnp.exp2(m - m_new) * acc + _dot(p, v_ref[pl.ds(k0, tk), :])
        return m_new, acc

    init = (jnp.full((rows, 1), NEG_INF, F32), jnp.zeros((rows, EXT), F32))
    _, acc = lax.fori_loop(0, k_ref.shape[0] // tk, chunk, init)
    _store_heads(o_ref, acc)


def dense_attention(q, k_ext, v_ext, shift, fast, tq=DENSE_TQ, tk=DENSE_TK, heads=DENSE_HEADS):
    t_len = q.shape[0]
    d = HEAD_DIM
    kp = k_ext.shape[0]
    assert GQA_GROUP % heads == 0
    groups_per_kv = GQA_GROUP // heads
    blk = 2 * _nbytes((tq, heads * d), BF16) + 2 * _nbytes((kp, EXT), BF16)
    if fast:
        body = functools.partial(_dense_fast_kernel, tk=tk)
    else:
        body = functools.partial(_dense_safe_kernel, tk=tk, n_keys=t_len)
    return pl.pallas_call(
        body,
        grid=(GA_Q_HEADS // heads, t_len // tq),
        in_specs=[pl.BlockSpec(memory_space=pltpu.SMEM),
                  pl.BlockSpec((tq, heads * d), lambda h, i: (i, h)),
                  pl.BlockSpec((kp, EXT), lambda h, i: (0, h // groups_per_kv)),
                  pl.BlockSpec((kp, EXT), lambda h, i: (0, h // groups_per_kv))],
        out_specs=pl.BlockSpec((tq, heads * d), lambda h, i: (i, h)),
        out_shape=jax.ShapeDtypeStruct((t_len, GA_Q_DIM), BF16),
        compiler_params=_params(blk, 10 * _nbytes((heads * tq, tk), F32)),
        name="dense_attention_fast" if fast else "dense_attention_safe",
    )(shift, q, k_ext, v_ext)


def _mixer_ab(xg, rstd, i, w_in, conv_w, conv_b, w_a, b_a, w_x, b_x, lam, sink):
    ug, _ = matmul(xg, rstd, w_in, 0, 2 * LRU_WIDTH, F32, tn=1024)
    qkv, _ = matmul(xg, rstd, w_in, 2 * LRU_WIDTH, SWA_Q_DIM + 2 * SWA_KV_DIM, BF16, tn=1024)
    y_a = rglru(ug, conv_w[i], conv_b[i], w_a[i], b_a[i], w_x[i], b_x[i], lam[i])
    y_b = windowed_attention(qkv, sink[i])
    return [y_a, y_b]


def _mixer_c(xg, rstd, i, w_in, q_norm, k_norm, cos, sin):
    qkv, _ = matmul(xg, rstd, w_in, 0, GA_Q_DIM + 2 * GA_KV_DIM, F32, tn=1024)
    q, k_ext, v_ext = qkv_prep(qkv, q_norm[i], k_norm[i], cos, sin)
    shift = (HEAD_DIM * jnp.max(jnp.abs(q_norm[i])) * jnp.max(jnp.abs(k_norm[i]))).reshape(1)
    o = lax.cond(ATTN_SCALE * shift[0] <= FAST_SOFTMAX_MAX_LOGIT,
                 lambda a: dense_attention(*a, fast=True),
                 lambda a: dense_attention(*a, fast=False), (q, k_ext, v_ext, shift))
    return [o]


def kernel(x, meta_tokens, ffn1_norm, ffn1_w_gu, ffn1_w_down, mix_norm, ab_w_in, ab_w_out, lru_conv_w, lru_conv_b, lru_w_a, lru_b_a, lru_w_x, lru_b_x, lru_lambda, swa_sink, c_w_in, c_w_out, c_q_norm, c_k_norm, ffn2_norm, ffn2_w_gu, ffn2_w_down, final_norm):
    bsz, n, d = x.shape
    assert (bsz, n, d) == (1, SEQ, D_MODEL)
    cos, sin = _rope_tables(T_TOK, KEYS_PAD)
    h, xg, rstd = embed_stats(x[0], meta_tokens.astype(x.dtype), ffn1_norm[0])
    w_gu1 = ffn1_w_gu[0].astype(BF16)
    for layer in range(DEPTH):
        i = layer // 2
        if layer % 2 == 0:
            in_job, out_job = (ab_w_in, i, 128, AB_IN // 2), (ab_w_out, i, 64, D_MODEL)
        else:
            in_job, out_job = (c_w_in, i, 128, C_IN // 3), (c_w_out, i, 64, D_MODEL)
        act, (w_dn1, w_in) = matmul_swiglu(xg, rstd, w_gu1,
                                           cast_jobs=[(ffn1_w_down, layer, 64, D_MODEL), in_job])
        (h, xg, rstd), (w_out,) = matmul_residual([act], w_dn1, h, 0.5, g_next=mix_norm[layer],
                                                  cast_jobs=[out_job])
        if layer % 2 == 0:
            ys = _mixer_ab(xg, rstd, i, w_in, lru_conv_w, lru_conv_b, lru_w_a, lru_b_a,
                           lru_w_x, lru_b_x, lru_lambda, swa_sink)
        else:
            ys = _mixer_c(xg, rstd, i, w_in, c_q_norm, c_k_norm, cos, sin)
        (h, xg, rstd), (w_gu2,) = matmul_residual(ys, w_out, h, 1.0, g_next=ffn2_norm[layer],
                                                  cast_jobs=[(ffn2_w_gu, layer, 64, 2 * D_FF)])
        last = layer + 1 == DEPTH
        jobs = [(ffn2_w_down, layer, 64, D_MODEL)]
        if not last:
            jobs.append((ffn1_w_gu, layer + 1, 128, D_MODEL))
        act, cast = matmul_swiglu(xg, rstd, w_gu2, cast_jobs=jobs)
        if last:
            h, _ = matmul_residual([act], cast[0], h, 0.5)
        else:
            w_gu1 = cast[1]
            (h, xg, rstd), _ = matmul_residual([act], cast[0], h, 0.5, g_next=ffn1_norm[layer + 1])
    return final_rmsnorm(h, final_norm)[None]
```

```python
import functools

import numpy as np
import jax
import jax.numpy as jnp
from jax import lax
from jax.experimental import pallas as pl
from jax.experimental.pallas import tpu as pltpu

F32 = jnp.float32
BF16 = jnp.bfloat16

D_MODEL = 4096
SEQ = 8192
DEPTH = 4
HEAD_DIM = 128
N_META = 16
T_TOK = SEQ + N_META
GRID_W = 64
BLOCK = 128
WINDOW = 128
RMS_EPS = 1e-6
NEG_INF = -1e30
D_FF = (3 * D_MODEL) // 2
LRU_WIDTH = D_MODEL // 2
LRU_C = 8.0
SWA_Q_HEADS = 16
SWA_KV_HEADS = 4
SWA_Q_DIM = SWA_Q_HEADS * HEAD_DIM
SWA_KV_DIM = SWA_KV_HEADS * HEAD_DIM
GA_Q_HEADS = 32
GA_KV_HEADS = 8
GA_Q_DIM = GA_Q_HEADS * HEAD_DIM
GA_KV_DIM = GA_KV_HEADS * HEAD_DIM
GQA_GROUP = 4
AB_IN = 2 * LRU_WIDTH + SWA_Q_DIM + 2 * SWA_KV_DIM
C_IN = GA_Q_DIM + 2 * GA_KV_DIM
ROPE_BASE = 10000.0
ATTN_SCALE = HEAD_DIM ** -0.5

V7X_VMEM_BYTES = 64 * 2**20
VMEM_CAP_BYTES = V7X_VMEM_BYTES - 6 * 2**20

TM = 912
LANES = 128
MXU_DIM = 256
KEYS_PAD = 33 * MXU_DIM
DENSE_TK = MXU_DIM
DENSE_TQ = 912
DENSE_HEADS = 2
PREP_TM = 384
NORM_TM = 432
LRU_L = 432
EXT = 2 * HEAD_DIM
EXP2_SCALE = ATTN_SCALE * float(np.log2(np.e))
PAD_KEY_PENALTY = 30000.0
FAST_SOFTMAX_MAX_LOGIT = 40.0


def _nbytes(shape, dtype):
    return int(np.prod(shape)) * jnp.dtype(dtype).itemsize


def _params(block_bytes, extra_bytes=0):
    need = 2 * block_bytes + extra_bytes + 4 * 2**20
    return pltpu.CompilerParams(vmem_limit_bytes=int(min(max(need, 16 * 2**20), VMEM_CAP_BYTES)))


def _embed_stats_kernel(meta_ref, tail_ref, main_ref, g_ref, h_ref, xg_ref, rstd_ref):
    tm = h_ref.shape[0]
    head = jnp.where(pl.program_id(0) == 0, meta_ref[...], tail_ref[...])
    x = jnp.concatenate([head, main_ref[:tm - N_META, :]], axis=0)
    h_ref[...] = x
    xg_ref[...] = (x * g_ref[...]).astype(xg_ref.dtype)
    rstd = lax.rsqrt(jnp.mean(x * x, axis=-1, keepdims=True) + RMS_EPS)
    rstd_ref[...] = jnp.broadcast_to(rstd, rstd_ref.shape)


def embed_stats(x, meta, g, tm=NORM_TM):
    n, d = x.shape
    m = n + N_META
    assert tm % N_META == 0 and m % tm == 0
    per = tm // N_META
    blk = (2 * _nbytes((tm, d), F32) + 2 * _nbytes((N_META, d), F32) + _nbytes((tm, d), BF16)
           + _nbytes((tm, LANES), F32))
    return pl.pallas_call(
        _embed_stats_kernel,
        grid=(m // tm,),
        in_specs=[pl.BlockSpec((N_META, d), lambda i: (0, 0)),
                  pl.BlockSpec((N_META, d), lambda i: (jnp.maximum(i * per - 1, 0), 0)),
                  pl.BlockSpec((tm, d), lambda i: (i, 0)),
                  pl.BlockSpec((1, d), lambda i: (0, 0))],
        out_specs=[pl.BlockSpec((tm, d), lambda i: (i, 0)),
                   pl.BlockSpec((tm, d), lambda i: (i, 0)),
                   pl.BlockSpec((tm, LANES), lambda i: (i, 0))],
        out_shape=[jax.ShapeDtypeStruct((m, d), F32),
                   jax.ShapeDtypeStruct((m, d), BF16),
                   jax.ShapeDtypeStruct((m, LANES), F32)],
        compiler_params=_params(blk, _nbytes((tm, d), F32)),
        name="embed_stats",
    )(meta, x, x, g.reshape(1, d))


def _final_norm_kernel(a_ref, b_ref, g_ref, o_ref):
    x = jnp.concatenate([a_ref[N_META:, :], b_ref[...]], axis=0)
    y = x * lax.rsqrt(jnp.mean(x * x, axis=-1, keepdims=True) + RMS_EPS)
    o_ref[...] = y * g_ref[...]


def final_rmsnorm(h, g, tm=256):
    m, d = h.shape
    n = m - N_META
    blk = 2 * _nbytes((tm, d), F32) + _nbytes((N_META, d), F32)
    return pl.pallas_call(
        _final_norm_kernel,
        grid=(n // tm,),
        in_specs=[pl.BlockSpec((tm, d), lambda i: (i, 0)),
                  pl.BlockSpec((N_META, d), lambda i: ((i + 1) * (tm // N_META), 0)),
                  pl.BlockSpec((1, d), lambda i: (0, 0))],
        out_specs=pl.BlockSpec((tm, d), lambda i: (i, 0)),
        out_shape=jax.ShapeDtypeStruct((n, d), F32),
        compiler_params=_params(blk, 2 * _nbytes((tm, d), F32)),
        name="final_norm",
    )(h, h, g.reshape(1, d))


def _dot(a, b):
    return jnp.dot(a, b, preferred_element_type=F32)


def _lane_tile(v, width):
    return jnp.concatenate([v] * (width // LANES), axis=1)


def _cast_specs(jobs, n_col_tiles, steps):
    in_specs, out_specs, out_shape, args, nbytes = [], [], [], [], 0
    for w, layer, rb, cb in jobs:
        _, k, n = w.shape
        assert k % rb == 0 and n % cb == 0 and (k // rb) * (n // cb) <= steps
        nc = n // cb
        last = (k // rb) * nc - 1

        def slab(i, j, nc=nc, last=last):
            t = jnp.minimum(i * n_col_tiles + j, last)
            return t // nc, t % nc

        in_specs.append(pl.BlockSpec((None, rb, cb),
                                     functools.partial(lambda i, j, f, l: (l, *f(i, j)), f=slab, l=layer)))
        out_specs.append(pl.BlockSpec((rb, cb), slab))
        out_shape.append(jax.ShapeDtypeStruct((k, n), BF16))
        args.append(w)
        nbytes += _nbytes((rb, cb), F32) + _nbytes((rb, cb), BF16)
    return in_specs, out_specs, out_shape, args, nbytes


def _run_cast_jobs(src_refs, dst_refs):
    for src, dst in zip(src_refs, dst_refs):
        dst[...] = src[...].astype(dst.dtype)


def _mm_kernel(*refs, n_cast):
    x_ref, rstd_ref, w_ref = refs[:3]
    o_ref = refs[3 + n_cast]
    _run_cast_jobs(refs[3:3 + n_cast], refs[4 + n_cast:])
    acc = _dot(x_ref[...], w_ref[...])
    o_ref[...] = (acc * _lane_tile(rstd_ref[...], acc.shape[1])).astype(o_ref.dtype)


def matmul(xg, rstd, w, n_off, n, out_dtype, tn, cast_jobs=(), tm=TM):
    m, k = xg.shape
    joff = n_off // tn
    grid = (m // tm, n // tn)
    c_in, c_out, c_shape, c_args, c_bytes = _cast_specs(cast_jobs, grid[1], grid[0] * grid[1])
    blk = (_nbytes((tm, k), BF16) + _nbytes((tm, LANES), F32) + _nbytes((k, tn), BF16)
           + _nbytes((tm, tn), out_dtype) + c_bytes)
    out = pl.pallas_call(
        functools.partial(_mm_kernel, n_cast=len(cast_jobs)),
        grid=grid,
        in_specs=[pl.BlockSpec((tm, k), lambda i, j: (i, 0)),
                  pl.BlockSpec((tm, LANES), lambda i, j: (i, 0)),
                  pl.BlockSpec((k, tn), lambda i, j: (0, j + joff)), *c_in],
        out_specs=[pl.BlockSpec((tm, tn), lambda i, j: (i, j)), *c_out],
        out_shape=[jax.ShapeDtypeStruct((m, n), out_dtype), *c_shape],
        compiler_params=_params(blk, 2 * _nbytes((tm, tn), F32)),
        name="matmul",
    )(xg, rstd, w, *c_args)
    return out[0], out[1:]


def _mm_swiglu_kernel(*refs, n_cast):
    x_ref, rstd_ref, wg_ref, wu_ref = refs[:4]
    o_ref = refs[4 + n_cast]
    _run_cast_jobs(refs[4:4 + n_cast], refs[5 + n_cast:])
    x = x_ref[...]
    rstd = _lane_tile(rstd_ref[...], o_ref.shape[1])
    gate = _dot(x, wg_ref[...]) * rstd
    up = _dot(x, wu_ref[...]) * rstd
    o_ref[...] = (gate * (0.5 + 0.5 * jnp.tanh(0.5 * gate)) * up).astype(o_ref.dtype)


def matmul_swiglu(xg, rstd, w_gu, cast_jobs=(), tn=512, tm=TM):
    m, k = xg.shape
    f = w_gu.shape[1] // 2
    uoff = f // tn
    grid = (f // tn, m // tm)
    c_in, c_out, c_shape, c_args, c_bytes = _cast_specs(cast_jobs, grid[1], grid[0] * grid[1])
    blk = (_nbytes((tm, k), BF16) + _nbytes((tm, LANES), F32) + 2 * _nbytes((k, tn), BF16)
           + _nbytes((tm, tn), BF16) + c_bytes)
    out = pl.pallas_call(
        functools.partial(_mm_swiglu_kernel, n_cast=len(cast_jobs)),
        grid=grid,
        in_specs=[pl.BlockSpec((tm, k), lambda j, i: (i, 0)),
                  pl.BlockSpec((tm, LANES), lambda j, i: (i, 0)),
                  pl.BlockSpec((k, tn), lambda j, i: (0, j)),
                  pl.BlockSpec((k, tn), lambda j, i: (0, j + uoff)), *c_in],
        out_specs=[pl.BlockSpec((tm, tn), lambda j, i: (i, j)), *c_out],
        out_shape=[jax.ShapeDtypeStruct((m, f), BF16), *c_shape],
        compiler_params=_params(blk, 3 * _nbytes((tm, tn), F32)),
        name="matmul_swiglu",
    )(xg, rstd, w_gu, w_gu, *c_args)
    return out[0], out[1:]


def _mm_res_kernel(*refs, n_x, n_cast, scale, with_stats, d_model):
    x_refs, w_refs = refs[:n_x], refs[n_x:2 * n_x]
    r_ref = refs[2 * n_x]
    n_in = 2 * n_x + 1 + int(with_stats)
    n_out = 3 if with_stats else 1
    outs = refs[n_in + n_cast:n_in + n_cast + n_out]
    _run_cast_jobs(refs[n_in:n_in + n_cast], refs[n_in + n_cast + n_out:n_in + 2 * n_cast + n_out])
    acc = _dot(x_refs[0][...], w_refs[0][...])
    for x_ref, w_ref in zip(x_refs[1:], w_refs[1:]):
        acc = acc + _dot(x_ref[...], w_ref[...])
    h = r_ref[...] + scale * acc
    if not with_stats:
        outs[0][...] = h
        return
    g_ref, ssq_ref = refs[n_in - 1], refs[-1]
    o_ref, xg_ref, rstd_ref = outs
    o_ref[...] = h
    xg_ref[...] = (h * g_ref[...]).astype(xg_ref.dtype)
    sq = h * h
    part = sq[:, 0:LANES]
    for c in range(1, sq.shape[1] // LANES):
        part = part + sq[:, c * LANES:(c + 1) * LANES]
    j = pl.program_id(1)

    @pl.when(j == 0)
    def _():
        ssq_ref[...] = part

    @pl.when(j > 0)
    def _():
        ssq_ref[...] += part

    @pl.when(j == pl.num_programs(1) - 1)
    def _():
        tot = jnp.sum(ssq_ref[...], axis=-1, keepdims=True)
        rstd_ref[...] = jnp.broadcast_to(lax.rsqrt(tot * (1.0 / d_model) + RMS_EPS), rstd_ref.shape)


def matmul_residual(xs, w, res, scale, g_next=None, cast_jobs=(), tn=512, tm=TM):
    m, k = xs[0].shape
    n_x = len(xs)
    assert all(x.shape == (m, k) for x in xs)
    n = w.shape[1]
    with_stats = g_next is not None
    grid = (m // tm, n // tn)
    c_in, c_out, c_shape, c_args, c_bytes = _cast_specs(cast_jobs, grid[1], grid[0] * grid[1])
    blk = (n_x * (_nbytes((tm, k), BF16) + _nbytes((k, tn), BF16)) + 2 * _nbytes((tm, tn), F32)
           + _nbytes((tm, tn), BF16) + _nbytes((tm, LANES), F32) + c_bytes)
    in_specs = [pl.BlockSpec((tm, k), lambda i, j: (i, 0)) for _ in xs]
    in_specs += [pl.BlockSpec((k, tn), functools.partial(lambda i, j, c: (c, j), c=c))
                 for c in range(n_x)]
    in_specs.append(pl.BlockSpec((tm, tn), lambda i, j: (i, j)))
    out_specs = [pl.BlockSpec((tm, tn), lambda i, j: (i, j))]
    out_shape = [jax.ShapeDtypeStruct((m, n), F32)]
    args = [*xs, *([w] * n_x), res]
    scratch = []
    if with_stats:
        in_specs.append(pl.BlockSpec((1, tn), lambda i, j: (0, j)))
        out_specs += [pl.BlockSpec((tm, tn), lambda i, j: (i, j)),
                      pl.BlockSpec((tm, LANES), lambda i, j: (i, 0))]
        out_shape += [jax.ShapeDtypeStruct((m, n), BF16), jax.ShapeDtypeStruct((m, LANES), F32)]
        args.append(g_next.reshape(1, n))
        scratch = [pltpu.VMEM((tm, LANES), F32)]
    n_main = len(out_shape)
    out = pl.pallas_call(
        functools.partial(_mm_res_kernel, n_x=n_x, n_cast=len(cast_jobs), scale=scale,
                          with_stats=with_stats, d_model=n),
        grid=grid,
        in_specs=[*in_specs, *c_in],
        out_specs=[*out_specs, *c_out],
        out_shape=[*out_shape, *c_shape],
        scratch_shapes=scratch,
        compiler_params=_params(blk, 3 * _nbytes((tm, tn), F32)),
        name="matmul_residual",
    )(*args, *c_args)
    main = out[:n_main]
    return (main if with_stats else main[0]), out[n_main:]


def _softplus(z):
    return jnp.maximum(z, 0.0) + jnp.log1p(jnp.exp(-jnp.abs(z)))


def _lru_kernel(u_ref, g_ref, cw_ref, cb_ref, wa_ref, ba_ref, wx_ref, bx_ref, lam_ref,
                o_ref, uc_ref, h_ref, *, chunk):
    t_len = u_ref.shape[0]
    n_chunks = t_len // chunk
    n_blk = chunk // 8
    cw = cw_ref[...]
    cb = cb_ref[...]
    row8 = lax.broadcasted_iota(jnp.int32, (8, HEAD_DIM), 0)

    def coeffs(uc, d):
        ucb = uc.astype(BF16)
        gate_r = _dot(ucb, wa_ref[d]) + ba_ref[d:d + 1, :]
        gate_i = _dot(ucb, wx_ref[d]) + bx_ref[d:d + 1, :]
        r = jax.nn.sigmoid(gate_r)
        i = 0.5 + 0.5 * jnp.tanh(0.5 * gate_i)
        log_a = (-LRU_C * r) * _softplus(-lam_ref[d:d + 1, :])
        a = jnp.exp(log_a)
        th = jnp.tanh(log_a)
        n = -2.0 * th
        coef = jnp.where(n > 0.0, n * lax.rsqrt(n * (1.0 - th)), 0.0)
        b = coef * (i * uc)
        return a, b

    def tile_scan(a, b, reverse):
        for s in (1, 2, 4):
            if reverse:
                keep = row8 < 8 - s
                shift = 8 - s
            else:
                keep = row8 >= s
                shift = s
            a_sh = jnp.where(keep, pltpu.roll(a, shift, 0), 1.0)
            b_sh = jnp.where(keep, pltpu.roll(b, shift, 0), 0.0)
            b = a * b_sh + b
            a = a * a_sh
        return a, b

    def fwd_chunk(c, carry):
        c0 = pl.multiple_of(c * chunk, 8)
        main = u_ref[pl.ds(c0, chunk), :]
        prev = u_ref[pl.ds(jnp.maximum(c0 - 8, 0), 8), :]
        prev = jnp.where(c > 0, prev, 0.0)
        nxt = u_ref[pl.ds(jnp.minimum(c0 + chunk, t_len - 8), 8), :]
        nxt = jnp.where(c < n_chunks - 1, nxt, 0.0)
        x = jnp.concatenate([prev, main, nxt], axis=0)
        uc = (x[6:6 + chunk] * cw[0:1] + x[7:7 + chunk] * cw[1:2]
              + x[8:8 + chunk] * cw[2:3] + x[9:9 + chunk] * cw[3:4]) + cb
        uc_ref[pl.ds(c0, chunk), :] = uc
        a, b = coeffs(uc, 0)
        for k in range(n_blk):
            at, bt = tile_scan(a[8 * k:8 * k + 8], b[8 * k:8 * k + 8], False)
            ht = at * carry + bt
            h_ref[pl.ds(c0 + 8 * k, 8), :] = ht
            carry = jnp.broadcast_to(ht[7:8, :], (8, HEAD_DIM))
        return carry

    lax.fori_loop(0, n_chunks, fwd_chunk, jnp.zeros((8, HEAD_DIM), F32))

    def rev_chunk(ci, carry):
        c = n_chunks - 1 - ci
        c0 = pl.multiple_of(c * chunk, 8)
        uc = uc_ref[pl.ds(c0, chunk), :]
        a, b = coeffs(uc, 1)
        for k in reversed(range(n_blk)):
            at, bt = tile_scan(a[8 * k:8 * k + 8], b[8 * k:8 * k + 8], True)
            ht = at * carry + bt
            rows = pl.ds(c0 + 8 * k, 8)
            h_ref[rows, :] = h_ref[rows, :] + ht
            carry = jnp.broadcast_to(ht[0:1, :], (8, HEAD_DIM))
        rows = pl.ds(c0, chunk)
        o_ref[rows, :] = (h_ref[rows, :] * jax.nn.gelu(g_ref[rows, :])).astype(o_ref.dtype)
        return carry

    lax.fori_loop(0, n_chunks, rev_chunk, jnp.zeros((8, HEAD_DIM), F32))


def rglru(ug, conv_w, conv_b, w_a, b_a, w_x, b_x, lam, chunk=LRU_L):
    t_len = ug.shape[0]
    width = ug.shape[1] // 2
    nb = width // HEAD_DIM
    c = HEAD_DIM
    blk = (2 * _nbytes((t_len, c), F32) + _nbytes((t_len, c), BF16)
           + 4 * _nbytes((c, c), BF16))
    scratch = 2 * _nbytes((t_len, c), F32)
    return pl.pallas_call(
        functools.partial(_lru_kernel, chunk=chunk),
        grid=(nb,),
        in_specs=[pl.BlockSpec((t_len, c), lambda n: (0, n)),
                  pl.BlockSpec((t_len, c), lambda n: (0, n + nb)),
                  pl.BlockSpec((4, c), lambda n: (0, n)),
                  pl.BlockSpec((1, c), lambda n: (0, n)),
                  pl.BlockSpec((2, None, c, c), lambda n: (0, n, 0, 0)),
                  pl.BlockSpec((2, c), lambda n: (0, n)),
                  pl.BlockSpec((2, None, c, c), lambda n: (0, n, 0, 0)),
                  pl.BlockSpec((2, c), lambda n: (0, n)),
                  pl.BlockSpec((2, c), lambda n: (0, n))],
        out_specs=pl.BlockSpec((t_len, c), lambda n: (0, n)),
        out_shape=jax.ShapeDtypeStruct((t_len, width), BF16),
        scratch_shapes=[pltpu.VMEM((t_len, c), F32), pltpu.VMEM((t_len, c), F32)],
        compiler_params=_params(blk, scratch + 16 * _nbytes((chunk, c), F32)),
        name="rglru",
    )(ug, ug, conv_w, conv_b.reshape(1, width), w_a.astype(BF16), b_a, w_x.astype(BF16), b_x, lam)


_NT = (((1,), (1,)), ((), ()))


def _sink_softmax_pv(s, sink, v):
    d = HEAD_DIM
    fold = s[:, 0:LANES]
    for c in range(1, s.shape[1] // LANES):
        fold = jnp.maximum(fold, s[:, c * LANES:(c + 1) * LANES])
    m = jnp.maximum(jnp.max(fold, axis=1, keepdims=True), sink)
    p = jnp.exp(s - m).astype(BF16)
    pv = _dot(p, jnp.concatenate([v, jnp.ones_like(v)], axis=1))
    return pv[:, :d] / (pv[:, d:] + jnp.exp(sink - m))


def _swa_kernel(sink_ref, slope_ref, q_ref, k_ref, v_ref, o_ref):
    h = pl.program_id(0)
    n_real = q_ref.shape[0] - N_META
    n_blocks = n_real // BLOCK
    d = HEAD_DIM
    zpad = jnp.zeros((BLOCK - N_META, d), BF16)
    k_meta = jnp.concatenate([k_ref[0:N_META, :], zpad], axis=0)
    v_meta = jnp.concatenate([v_ref[0:N_META, :], zpad], axis=0)
    kw = 3 * BLOCK

    def per_head_rows(ref, rows):
        return jnp.concatenate([jnp.full((rows, 1), ref[GQA_GROUP * h + g], F32)
                                for g in range(GQA_GROUP)], axis=0)

    def stack_heads(x):
        return jnp.concatenate([x[:, g * d:(g + 1) * d] for g in range(GQA_GROUP)], axis=0)

    sink = per_head_rows(sink_ref, BLOCK)
    neg_slope = -per_head_rows(slope_ref, BLOCK)
    col = lax.broadcasted_iota(jnp.int32, (GQA_GROUP * BLOCK, BLOCK + kw), 1)
    rowq = lax.broadcasted_iota(jnp.int32, (GQA_GROUP * BLOCK, BLOCK + kw), 0) % BLOCK
    is_meta = col < N_META
    in_cols = col >= BLOCK
    col_off = col - BLOCK - rowq

    def bias_for(first_key_minus_first_query):
        rel = jnp.abs(col_off + first_key_minus_first_query)
        in_win = in_cols & (rel <= WINDOW)
        return jnp.where(is_meta, 0.0, jnp.where(in_win, neg_slope * rel.astype(F32), NEG_INF))

    def attend(j, ws, bias):
        r0, w0 = N_META + BLOCK * j, N_META + ws
        if not isinstance(j, int):
            r0, w0 = pl.multiple_of(r0, N_META), pl.multiple_of(w0, N_META)
        kcat = jnp.concatenate([k_meta, k_ref[pl.ds(w0, kw), :]], axis=0)
        vcat = jnp.concatenate([v_meta, v_ref[pl.ds(w0, kw), :]], axis=0)
        qs = stack_heads(q_ref[pl.ds(r0, BLOCK), :])
        s = lax.dot_general(qs, kcat, _NT, preferred_element_type=F32) * ATTN_SCALE + bias
        o = _sink_softmax_pv(s, sink, vcat)
        for g in range(GQA_GROUP):
            o_ref[pl.ds(r0, BLOCK), g * d:(g + 1) * d] = o[g * BLOCK:(g + 1) * BLOCK].astype(o_ref.dtype)

    attend(0, 0, bias_for(0))
    bias_mid = bias_for(-BLOCK)

    def block(j, carry):
        attend(j, BLOCK * (j - 1), bias_mid)
        return carry

    lax.fori_loop(1, n_blocks - 1, block, 0, unroll=4)
    attend(n_blocks - 1, n_real - kw, bias_for(n_real - kw - BLOCK * (n_blocks - 1)))

    kcat = jnp.concatenate([k_meta, k_ref[N_META:N_META + BLOCK, :]], axis=0)
    vcat = jnp.concatenate([v_meta, v_ref[N_META:N_META + BLOCK, :]], axis=0)
    colm = lax.broadcasted_iota(jnp.int32, (GQA_GROUP * N_META, 2 * BLOCK), 1)
    rowm = lax.broadcasted_iota(jnp.int32, (GQA_GROUP * N_META, 2 * BLOCK), 0) % N_META
    ok = (colm < N_META) | ((colm >= BLOCK) & (colm - BLOCK + N_META - rowm <= WINDOW))
    bias = jnp.where(ok, 0.0, NEG_INF)
    qs = stack_heads(q_ref[0:N_META, :])
    s = lax.dot_general(qs, kcat, _NT, preferred_element_type=F32) * ATTN_SCALE + bias
    o = _sink_softmax_pv(s, per_head_rows(sink_ref, N_META), vcat)
    for g in range(GQA_GROUP):
        o_ref[0:N_META, g * d:(g + 1) * d] = o[g * N_META:(g + 1) * N_META].astype(o_ref.dtype)


def windowed_attention(qkv, sink):
    t_len = qkv.shape[0]
    d = HEAD_DIM
    gw = GQA_GROUP * d
    slopes = jnp.asarray(2.0 ** (-8.0 * np.arange(1, SWA_Q_HEADS + 1) / SWA_Q_HEADS), F32)
    k_blk0 = SWA_Q_DIM // d
    v_blk0 = (SWA_Q_DIM + SWA_KV_DIM) // d
    blk = 2 * _nbytes((t_len, gw), BF16) + 2 * _nbytes((t_len, d), BF16)
    smem = pl.BlockSpec(memory_space=pltpu.SMEM)
    return pl.pallas_call(
        _swa_kernel,
        grid=(SWA_KV_HEADS,),
        in_specs=[smem, smem,
                  pl.BlockSpec((t_len, gw), lambda h: (0, h)),
                  pl.BlockSpec((t_len, d), lambda h: (0, k_blk0 + h)),
                  pl.BlockSpec((t_len, d), lambda h: (0, v_blk0 + h))],
        out_specs=pl.BlockSpec((t_len, gw), lambda h: (0, h)),
        out_shape=jax.ShapeDtypeStruct((t_len, SWA_Q_DIM), BF16),
        compiler_params=_params(blk, 8 * 2**20),
        name="windowed_attention",
    )(sink.astype(F32), slopes, qkv, qkv, qkv)


def _rope_tables(t_len, rows_pad):
    half = HEAD_DIM // 2
    n = t_len - N_META
    rows = n // GRID_W
    row = jnp.concatenate([jnp.zeros((N_META,), jnp.int32),
                           jnp.repeat(jnp.arange(rows, dtype=jnp.int32), GRID_W)])
    col = jnp.concatenate([jnp.zeros((N_META,), jnp.int32),
                           jnp.tile(jnp.arange(GRID_W, dtype=jnp.int32), rows)])
    inv_freq = ROPE_BASE ** (-(jnp.arange(half // 2, dtype=F32) * 2.0 / half))
    ang_row = row.astype(F32)[:, None] * inv_freq
    ang_col = col.astype(F32)[:, None] * inv_freq
    cos = jnp.concatenate([jnp.cos(ang_row), jnp.cos(ang_row), jnp.cos(ang_col), jnp.cos(ang_col)], axis=1)
    sin = jnp.concatenate([-jnp.sin(ang_row), jnp.sin(ang_row), -jnp.sin(ang_col), jnp.sin(ang_col)], axis=1)
    pad = ((0, rows_pad - t_len), (0, 0))
    return jnp.pad(cos, pad), jnp.pad(sin, pad)


def _qkv_prep_kernel(x_ref, cos_ref, sin_ref, qn_ref, kn_ref, q_ref, k_ref, v_ref, *, t_len):
    tm = x_ref.shape[0]
    d = HEAD_DIM
    cos = cos_ref[...]
    sin = sin_ref[...]
    lane = lax.broadcasted_iota(jnp.int32, (tm, d), 1)
    first = (lane % (d // 2)) < (d // 4)
    rows = pl.program_id(0) * tm + lax.broadcasted_iota(jnp.int32, (tm, 1), 0)
    valid = rows < t_len

    ones_rhs = jnp.ones((3 * d, d), BF16)

    def lane_sum(v):
        hi = v.astype(BF16)
        r1 = v - hi.astype(F32)
        mid = r1.astype(BF16)
        lo = (r1 - mid.astype(F32)).astype(BF16)
        return _dot(jnp.concatenate([hi, mid, lo], axis=1), ones_rhs)

    def norm_rope(xh, g):
        y = xh * lax.rsqrt(lane_sum(xh * xh) * (1.0 / d) + RMS_EPS) * g
        partner = jnp.where(first, pltpu.roll(y, d - d // 4, 1), pltpu.roll(y, d // 4, 1))
        return y * cos + partner * sin

    qn = qn_ref[...]
    kn = kn_ref[...]
    for hd in range(GA_Q_HEADS):
        q_ref[:, hd * d:(hd + 1) * d] = norm_rope(x_ref[:, hd * d:(hd + 1) * d], qn).astype(q_ref.dtype)
    k_aux = jnp.where(lane == 0, 1.0, jnp.where((lane == 1) & jnp.logical_not(valid), 1.0, 0.0))
    ones = jnp.ones((tm, d), v_ref.dtype)
    for hd in range(GA_KV_HEADS):
        kk = norm_rope(x_ref[:, GA_Q_DIM + hd * d:GA_Q_DIM + (hd + 1) * d], kn)
        k_ref[:, hd * EXT:hd * EXT + d] = jnp.where(valid, kk, 0.0).astype(k_ref.dtype)
        k_ref[:, hd * EXT + d:(hd + 1) * EXT] = k_aux.astype(k_ref.dtype)
        v = x_ref[:, GA_Q_DIM + GA_KV_DIM + hd * d:GA_Q_DIM + GA_KV_DIM + (hd + 1) * d]
        v_ref[:, hd * EXT:hd * EXT + d] = jnp.where(valid, v, 0.0).astype(v_ref.dtype)
        v_ref[:, hd * EXT + d:(hd + 1) * EXT] = ones


def qkv_prep(qkv, q_norm, k_norm, cos, sin, keys_pad=KEYS_PAD, tm=PREP_TM):
    t_len, width = qkv.shape
    d = HEAD_DIM
    blk = (_nbytes((tm, width), F32) + 2 * _nbytes((tm, d), F32)
           + _nbytes((tm, GA_Q_DIM), BF16) + 2 * _nbytes((tm, GA_KV_HEADS * EXT), BF16))
    return pl.pallas_call(
        functools.partial(_qkv_prep_kernel, t_len=t_len),
        grid=(keys_pad // tm,),
        in_specs=[pl.BlockSpec((tm, width), lambda i: (i, 0)),
                  pl.BlockSpec((tm, d), lambda i: (i, 0)),
                  pl.BlockSpec((tm, d), lambda i: (i, 0)),
                  pl.BlockSpec((1, d), lambda i: (0, 0)),
                  pl.BlockSpec((1, d), lambda i: (0, 0))],
        out_specs=[pl.BlockSpec((tm, GA_Q_DIM), lambda i: (i, 0)),
                   pl.BlockSpec((tm, GA_KV_HEADS * EXT), lambda i: (i, 0)),
                   pl.BlockSpec((tm, GA_KV_HEADS * EXT), lambda i: (i, 0))],
        out_shape=[jax.ShapeDtypeStruct((t_len, GA_Q_DIM), BF16),
                   jax.ShapeDtypeStruct((keys_pad, GA_KV_HEADS * EXT), BF16),
                   jax.ShapeDtypeStruct((keys_pad, GA_KV_HEADS * EXT), BF16)],
        compiler_params=_params(blk, 4 * 2**20),
        name="qkv_prep",
    )(qkv, cos, sin, q_norm.reshape(1, d), k_norm.reshape(1, d))


def _stacked_q_ext(q_ref, shift_ref):
    d = HEAD_DIM
    q = jnp.concatenate([q_ref[:, g * d:(g + 1) * d] for g in range(q_ref.shape[1] // d)], axis=0)
    lane = lax.broadcasted_iota(jnp.int32, q.shape, 1)
    aux = jnp.where(lane == 0, -shift_ref[0], jnp.where(lane == 1, -PAD_KEY_PENALTY, 0.0))
    return jnp.concatenate([q, aux.astype(q.dtype)], axis=1)


def _store_heads(o_ref, acc):
    d = HEAD_DIM
    tq = o_ref.shape[0]
    o = acc[:, :d] / acc[:, d:]
    for g in range(o_ref.shape[1] // d):
        o_ref[:, g * d:(g + 1) * d] = o[g * tq:(g + 1) * tq].astype(o_ref.dtype)


def _dense_fast_kernel(shift_ref, q_ref, k_ref, v_ref, o_ref, *, tk):
    q = _stacked_q_ext(q_ref, shift_ref)
    acc = None
    for c in range(k_ref.shape[0] // tk):
        s = lax.dot_general(q, k_ref[c * tk:(c + 1) * tk, :], _NT, preferred_element_type=F32)
        p = jnp.exp2(s * EXP2_SCALE).astype(BF16)
        pv = _dot(p, v_ref[c * tk:(c + 1) * tk, :])
        acc = pv if acc is None else acc + pv
    _store_heads(o_ref, acc)


def _dense_safe_kernel(shift_ref, q_ref, k_ref, v_ref, o_ref, *, tk, n_keys):
    q = _stacked_q_ext(q_ref, shift_ref)
    rows = q.shape[0]

    def chunk(c, carry):
        m, acc = carry
        k0 = pl.multiple_of(c * tk, tk)
        t = lax.dot_general(q, k_ref[pl.ds(k0, tk), :], _NT, preferred_element_type=F32) * EXP2_SCALE
        key = k0 + lax.broadcasted_iota(jnp.int32, (rows, tk), 1)
        t = jnp.where(key < n_keys, t, NEG_INF)
        m_new = jnp.maximum(m, jnp.max(t, axis=1, keepdims=True))
        p = jnp.exp2(t - m_new).astype(BF16)
        acc = jnp.exp2(m - m_new) * acc + _dot(p, v_ref[pl.ds(k0, tk), :])
        return m_new, acc

    init = (jnp.full((rows, 1), NEG_INF, F32), jnp.zeros((rows, EXT), F32))
    _, acc = lax.fori_loop(0, k_ref.shape[0] // tk, chunk, init)
    _store_heads(o_ref, acc)


def dense_attention(q, k_ext, v_ext, shift, fast, tq=DENSE_TQ, tk=DENSE_TK, heads=DENSE_HEADS):
    t_len = q.shape[0]
    d = HEAD_DIM
    kp = k_ext.shape[0]
    assert GQA_GROUP % heads == 0
    groups_per_kv = GQA_GROUP // heads
    blk = 2 * _nbytes((tq, heads * d), BF16) + 2 * _nbytes((kp, EXT), BF16)
    if fast:
        body = functools.partial(_dense_fast_kernel, tk=tk)
    else:
        body = functools.partial(_dense_safe_kernel, tk=tk, n_keys=t_len)
    return pl.pallas_call(
        body,
        grid=(GA_Q_HEADS // heads, t_len // tq),
        in_specs=[pl.BlockSpec(memory_space=pltpu.SMEM),
                  pl.BlockSpec((tq, heads * d), lambda h, i: (i, h)),
                  pl.BlockSpec((kp, EXT), lambda h, i: (0, h // groups_per_kv)),
                  pl.BlockSpec((kp, EXT), lambda h, i: (0, h // groups_per_kv))],
        out_specs=pl.BlockSpec((tq, heads * d), lambda h, i: (i, h)),
        out_shape=jax.ShapeDtypeStruct((t_len, GA_Q_DIM), BF16),
        compiler_params=_params(blk, 10 * _nbytes((heads * tq, tk), F32)),
        name="dense_attention_fast" if fast else "dense_attention_safe",
    )(shift, q, k_ext, v_ext)


def _mixer_ab(xg, rstd, i, w_in, conv_w, conv_b, w_a, b_a, w_x, b_x, lam, sink):
    ug, _ = matmul(xg, rstd, w_in, 0, 2 * LRU_WIDTH, F32, tn=1024)
    qkv, _ = matmul(xg, rstd, w_in, 2 * LRU_WIDTH, SWA_Q_DIM + 2 * SWA_KV_DIM, BF16, tn=1024)
    y_a = rglru(ug, conv_w[i], conv_b[i], w_a[i], b_a[i], w_x[i], b_x[i], lam[i])
    y_b = windowed_attention(qkv, sink[i])
    return [y_a, y_b]


def _mixer_c(xg, rstd, i, w_in, q_norm, k_norm, cos, sin):
    qkv, _ = matmul(xg, rstd, w_in, 0, GA_Q_DIM + 2 * GA_KV_DIM, F32, tn=1024)
    q, k_ext, v_ext = qkv_prep(qkv, q_norm[i], k_norm[i], cos, sin)
    shift = (HEAD_DIM * jnp.max(jnp.abs(q_norm[i])) * jnp.max(jnp.abs(k_norm[i]))).reshape(1)
    o = lax.cond(ATTN_SCALE * shift[0] <= FAST_SOFTMAX_MAX_LOGIT,
                 lambda a: dense_attention(*a, fast=True),
                 lambda a: dense_attention(*a, fast=False), (q, k_ext, v_ext, shift))
    return [o]


def kernel(x, meta_tokens, ffn1_norm, ffn1_w_gu, ffn1_w_down, mix_norm, ab_w_in, ab_w_out, lru_conv_w, lru_conv_b, lru_w_a, lru_b_a, lru_w_x, lru_b_x, lru_lambda, swa_sink, c_w_in, c_w_out, c_q_norm, c_k_norm, ffn2_norm, ffn2_w_gu, ffn2_w_down, final_norm):
    bsz, n, d = x.shape
    assert (bsz, n, d) == (1, SEQ, D_MODEL)
    cos, sin = _rope_tables(T_TOK, KEYS_PAD)
    h, xg, rstd = embed_stats(x[0], meta_tokens.astype(x.dtype), ffn1_norm[0])
    w_gu1 = ffn1_w_gu[0].astype(BF16)
    for layer in range(DEPTH):
        i = layer // 2
        if layer % 2 == 0:
            in_job, out_job = (ab_w_in, i, 128, AB_IN // 2), (ab_w_out, i, 64, D_MODEL)
        else:
            in_job, out_job = (c_w_in, i, 128, C_IN // 3), (c_w_out, i, 64, D_MODEL)
        act, (w_dn1, w_in) = matmul_swiglu(xg, rstd, w_gu1,
                                           cast_jobs=[(ffn1_w_down, layer, 64, D_MODEL), in_job])
        (h, xg, rstd), (w_out,) = matmul_residual([act], w_dn1, h, 0.5, g_next=mix_norm[layer],
                                                  cast_jobs=[out_job])
        if layer % 2 == 0:
            ys = _mixer_ab(xg, rstd, i, w_in, lru_conv_w, lru_conv_b, lru_w_a, lru_b_a,
                           lru_w_x, lru_b_x, lru_lambda, swa_sink)
        else:
            ys = _mixer_c(xg, rstd, i, w_in, c_q_norm, c_k_norm, cos, sin)
        (h, xg, rstd), (w_gu2,) = matmul_residual(ys, w_out, h, 1.0, g_next=ffn2_norm[layer],
                                                  cast_jobs=[(ffn2_w_gu, layer, 64, 2 * D_FF)])
        last = layer + 1 == DEPTH
        jobs = [(ffn2_w_down, layer, 64, D_MODEL)]
        if not last:
            jobs.append((ffn1_w_gu, layer + 1, 128, D_MODEL))
        act, cast = matmul_swiglu(xg, rstd, w_gu2, cast_jobs=jobs)
        if last:
            h, _ = matmul_residual([act], cast[0], h, 0.5)
        else:
            w_gu1 = cast[1]
            (h, xg, rstd), _ = matmul_residual([act], cast[0], h, 0.5, g_next=ffn1_norm[layer + 1])
    return final_rmsnorm(h, final_norm)[None]
```
